```python
import math
import jax, jax.numpy as jnp
from jax import lax
import numpy as np


D_MODEL = 1024
BATCH = 8
SEQ = 2048
DEPTH = 2

N_A_LAYERS = DEPTH // 2
N_B_LAYERS = DEPTH - N_A_LAYERS
RET_HEADS = 4
RET_QK_DIM = D_MODEL // RET_HEADS
RET_V_DIM = 2 * RET_QK_DIM
RET_CHUNK = 128
DIFF_HEADS = 4
DIFF_HEAD_DIM = D_MODEL // (2 * DIFF_HEADS)
DIFF_V_DIM = 2 * DIFF_HEAD_DIM
Q_BLOCK = 128
D_FF = 2816
CONV_WIDTH = 3
ROPE_THETA = 10000.0
NORM_EPS = 1e-6

kernel_name = 'yoco_retention_diffattn_convffn_adaln'


def rms_norm(x, gain=None, eps=NORM_EPS):
    xf = x.astype(jnp.float32)
    y = xf * lax.rsqrt(jnp.mean(xf * xf, axis=-1, keepdims=True) + eps)
    if gain is not None:
        y = y * gain.astype(jnp.float32)
    return y.astype(x.dtype)


def rope(x, freqs):
    s, d = x.shape[1], x.shape[-1]
    ang = jnp.arange(s, dtype=jnp.float32)[:, None] * freqs[None, :]
    shape = (s,) + (1,) * (x.ndim - 3) + (d // 2,)
    cos = jnp.cos(ang).reshape(shape)
    sin = jnp.sin(ang).reshape(shape)
    xf = x.astype(jnp.float32)
    x1, x2 = xf[..., : d // 2], xf[..., d // 2:]
    return jnp.concatenate([x1 * cos - x2 * sin, x2 * cos + x1 * sin], axis=-1).astype(x.dtype)


def adaln(c, w, b):
    return jax.nn.silu(c) @ w + b


def retention(q, k, v):
    b, s, h, dk = q.shape
    dv = v.shape[-1]
    c = RET_CHUNK
    n = s // c
    log_gamma = jnp.log(1.0 - 2.0 ** (-5.0 - jnp.arange(h, dtype=jnp.float32)))
    i = jnp.arange(c, dtype=jnp.float32)
    diff = i[:, None] - i[None, :]
    dmask = jnp.where(diff >= 0, jnp.exp(log_gamma[:, None, None] * jnp.maximum(diff, 0.0)), 0.0)
    q_dec = jnp.exp(log_gamma[:, None] * (i + 1.0))
    k_dec = jnp.exp(log_gamma[:, None] * (c - 1.0 - i))
    c_dec = jnp.exp(log_gamma * c)

    def to_chunks(t):
        return t.astype(jnp.float32).reshape(b, n, c, h, t.shape[-1]).transpose(1, 0, 3, 2, 4)

    def step(state, xs):
        qc, kc, vc = xs
        inner = jnp.einsum('bhqd,bhkd->bhqk', qc, kc) * dmask
        out = (jnp.einsum('bhqk,bhkv->bhqv', inner, vc)
               + jnp.einsum('bhqd,bhdv->bhqv', qc * q_dec[..., None], state))
        state = state * c_dec[:, None, None] + jnp.einsum('bhkd,bhkv->bhdv', kc * k_dec[..., None], vc)
        return state, out

    state0 = jnp.zeros((b, h, dk, dv), jnp.float32)
    _, out = lax.scan(step, state0, (to_chunks(q), to_chunks(k), to_chunks(v)))
    return out.transpose(1, 0, 3, 2, 4).reshape(b, s, h, dv).astype(v.dtype)


def diff_attention(q, k, v, lam):
    b, s, h, _, d = q.shape
    dv = v.shape[-1]
    nb = s // Q_BLOCK
    qb = q.reshape(b, nb, Q_BLOCK, h, 2, d).transpose(1, 0, 3, 4, 2, 5)
    kt = k.transpose(0, 2, 3, 1, 4)
    vt = v.transpose(0, 2, 1, 3)
    kpos = jnp.arange(s, dtype=jnp.int32)
    scale = d ** -0.5

    def block(args):
        qblk, start = args
        qpos = start + jnp.arange(Q_BLOCK, dtype=jnp.int32)
        sc = jnp.einsum('bhpqd,bhpkd->bhpqk', qblk, kt).astype(jnp.float32) * scale
        sc = jnp.where(kpos[None, :] <= qpos[:, None], sc, -jnp.inf)
        p = jax.nn.softmax(sc, axis=-1)
        a = p[:, :, 0] - lam * p[:, :, 1]
        return jnp.einsum('bhqk,bhkv->bhqv', a.astype(vt.dtype), vt)

    starts = jnp.arange(nb, dtype=jnp.int32) * Q_BLOCK
    out = lax.map(block, (qb, starts))
    return out.transpose(1, 0, 3, 2, 4).reshape(b, s, h, dv)


def conv_ffn(h, w_in, w_conv, b_conv, w_down):
    a, g = jnp.split(h @ w_in, 2, axis=-1)
    a = lax.conv_general_dilated(a, w_conv[:, None, :], window_strides=(1,),
                                 padding=[(CONV_WIDTH - 1, 0)],
                                 dimension_numbers=('NWC', 'WIO', 'NWC'),
                                 feature_group_count=D_FF) + b_conv
    return (jax.nn.gelu(a, approximate=False) * g) @ w_down


def setup_inputs(seed: int = 0) -> dict:
    key = jax.random.key(seed)
    ks = jax.random.split(key, 20)
    f32 = jnp.float32
    D = D_MODEL

    def nrm(k, shape, scale):
        return jax.random.normal(k, shape, f32) * scale

    ret_in = 2 * RET_HEADS * RET_QK_DIM + 2 * RET_HEADS * RET_V_DIM
    ret_out = RET_HEADS * RET_V_DIM
    diff_q = DIFF_HEADS * 2 * DIFF_HEAD_DIM
    kv = diff_q + DIFF_HEADS * DIFF_V_DIM
    diff_o = DIFF_HEADS * DIFF_V_DIM
    return {
        'x': nrm(ks[0], (BATCH, SEQ, D), 1.0),
        'c': nrm(ks[1], (BATCH, D), 1.0),
        'norm_gain': 1.0 + nrm(ks[2], (DEPTH, 2, D), 0.05),
        'w_ada': nrm(ks[3], (DEPTH, D, 6 * D), D ** -0.5),
        'b_ada': nrm(ks[4], (DEPTH, 6 * D), 0.02),
        'ret_w_in': nrm(ks[5], (N_A_LAYERS, D, ret_in), D ** -0.5),
        'ret_w_out': nrm(ks[6], (N_A_LAYERS, ret_out, D), ret_out ** -0.5),
        'ffn_w_in': nrm(ks[7], (DEPTH, D, 2 * D_FF), D ** -0.5),
        'ffn_w_conv': nrm(ks[8], (DEPTH, CONV_WIDTH, D_FF), CONV_WIDTH ** -0.5),
        'ffn_b_conv': nrm(ks[9], (DEPTH, D_FF), 0.02),
        'ffn_w_down': nrm(ks[10], (DEPTH, D_FF, D), D_FF ** -0.5),
        'kv_norm_gain': 1.0 + nrm(ks[11], (D,), 0.05),
        'kv_w_ada': nrm(ks[12], (D, 2 * D), D ** -0.5),
        'kv_b_ada': nrm(ks[13], (2 * D,), 0.02),
        'w_kv': nrm(ks[14], (D, kv), D ** -0.5),
        'diff_w_q': nrm(ks[15], (N_B_LAYERS, D, diff_q), D ** -0.5),
        'diff_lambda': nrm(ks[16], (N_B_LAYERS, 4, DIFF_HEAD_DIM), 0.1),
        'diff_subln_gain': 1.0 + nrm(ks[17], (N_B_LAYERS, DIFF_V_DIM), 0.05),
        'diff_w_out': nrm(ks[18], (N_B_LAYERS, diff_o, D), diff_o ** -0.5),
        'final_norm_gain': 1.0 + nrm(ks[19], (D,), 0.05),
    }


def reference(x, c, norm_gain, w_ada, b_ada, ret_w_in, ret_w_out, ffn_w_in, ffn_w_conv,
              ffn_b_conv, ffn_w_down, kv_norm_gain, kv_w_ada, kv_b_ada, w_kv, diff_w_q,
              diff_lambda, diff_subln_gain, diff_w_out, final_norm_gain):
    b, s, _ = x.shape
    ret_freqs = 1.0 / (ROPE_THETA ** jnp.linspace(0.0, 1.0, RET_QK_DIM // 2))
    diff_freqs = 1.0 / (ROPE_THETA ** (jnp.arange(0, DIFF_HEAD_DIM, 2, dtype=jnp.float32) / DIFF_HEAD_DIM))
    ret_split = [RET_HEADS * RET_QK_DIM, 2 * RET_HEADS * RET_QK_DIM,
                 2 * RET_HEADS * RET_QK_DIM + RET_HEADS * RET_V_DIM]
    k_dim = DIFF_HEADS * 2 * DIFF_HEAD_DIM
    k_sh = None
    v_sh = None
    for layer in range(DEPTH):
        mod = adaln(c, w_ada[layer], b_ada[layer])[:, None, :]
        sh1, sc1, g1, sh2, sc2, g2 = jnp.split(mod, 6, axis=-1)
        h = rms_norm(x, norm_gain[layer, 0]) * (1.0 + sc1) + sh1
        if layer < N_A_LAYERS:
            q, k, v, gt = jnp.split(h @ ret_w_in[layer], ret_split, axis=-1)
            q = rope(q.reshape(b, s, RET_HEADS, RET_QK_DIM), ret_freqs)
            k = rope(k.reshape(b, s, RET_HEADS, RET_QK_DIM), ret_freqs) * (RET_QK_DIM ** -0.5)
            o = retention(q, k, v.reshape(b, s, RET_HEADS, RET_V_DIM))
            o = rms_norm(o).reshape(b, s, RET_HEADS * RET_V_DIM)
            mix = (jax.nn.silu(gt) * o) @ ret_w_out[layer]
        else:
            j = layer - N_A_LAYERS
            q = rope((h @ diff_w_q[j]).reshape(b, s, DIFF_HEADS, 2, DIFF_HEAD_DIM), diff_freqs)
            lv = diff_lambda[j].astype(jnp.float32)
            lam_init = 0.8 - 0.6 * math.exp(-0.3 * layer)
            lam = jnp.exp(jnp.sum(lv[0] * lv[1])) - jnp.exp(jnp.sum(lv[2] * lv[3])) + lam_init
            o = diff_attention(q, k_sh, v_sh, lam)
            o = rms_norm(o, diff_subln_gain[j]) * (1.0 - lam_init)
            mix = o.reshape(b, s, DIFF_HEADS * DIFF_V_DIM) @ diff_w_out[j]
        x = x + g1 * mix
        h = rms_norm(x, norm_gain[layer, 1]) * (1.0 + sc2) + sh2
        x = x + g2 * conv_ffn(h, ffn_w_in[layer], ffn_w_conv[layer], ffn_b_conv[layer], ffn_w_down[layer])
        if layer == N_A_LAYERS - 1:
            kv_sh, kv_sc = jnp.split(adaln(c, kv_w_ada, kv_b_ada)[:, None, :], 2, axis=-1)
            hk = rms_norm(x, kv_norm_gain) * (1.0 + kv_sc) + kv_sh
            kv = hk @ w_kv
            k_sh = rope(kv[..., :k_dim].reshape(b, s, DIFF_HEADS, 2, DIFF_HEAD_DIM), diff_freqs)
            v_sh = kv[..., k_dim:].reshape(b, s, DIFF_HEADS, DIFF_V_DIM)
    return rms_norm(x, final_norm_gain)
```

```python
import functools
import math

import jax
import jax.numpy as jnp
from jax import lax
from jax.experimental import pallas as pl
from jax.experimental.pallas import tpu as pltpu

F32 = jnp.float32
BF16 = jnp.bfloat16

RET_HEADS = 4
RET_CHUNK = 128
DIFF_HEADS = 4
DIFF_HEAD_DIM = 128
DIFF_V_DIM = 256
CONV_WIDTH = 3
ROPE_THETA = 10000.0
NORM_EPS = 1e-6

LANES = 128
SUBLANES = 8
BF16_ROWS = 16
VMEM_LIMIT_BYTES = 56 * 1024 * 1024

ADA_TN = 512
PROJ_TM = 512
PROJ_TN = 512
RET_KCHUNK = 256
OUT_TM = 512
FFN_TM = 512
FFN_TF = 256
ATT_TQ = 256


def _cparams(semantics):
    return pltpu.CompilerParams(dimension_semantics=semantics,
                                vmem_limit_bytes=VMEM_LIMIT_BYTES)


def _rms(x):
    return x * lax.rsqrt(jnp.mean(x * x, axis=-1, keepdims=True) + NORM_EPS)


def _prenorm(x, gain, sc, sh):
    return (_rms(x) * gain) * (1.0 + sc) + sh


def _sigmoid(x):
    return 1.0 / (1.0 + jnp.exp(-x))


def _ada_kernel(c_ref, w_ref, b_ref, o_ref):
    c = c_ref[...]
    s = (c * _sigmoid(c)).astype(BF16)
    o_ref[...] = jnp.dot(s, w_ref[...].astype(BF16), preferred_element_type=F32) + b_ref[...]


def _ada(c, w, b, layer):
    bsz, d = c.shape
    n = w.shape[-1]
    return pl.pallas_call(
        _ada_kernel,
        grid=(n // ADA_TN,),
        in_specs=[
            pl.BlockSpec((bsz, d), lambda j: (0, 0)),
            pl.BlockSpec((None, d, ADA_TN), lambda j: (layer, 0, j)),
            pl.BlockSpec((None, 1, ADA_TN), lambda j: (layer, 0, j)),
        ],
        out_specs=pl.BlockSpec((bsz, ADA_TN), lambda j: (0, j)),
        out_shape=jax.ShapeDtypeStruct((bsz, n), F32),
        compiler_params=_cparams(("arbitrary",)),
        name="ada",
    )(c, w, b)


def _proj_kernel(x_ref, gain_ref, sc_ref, sh_ref, w_ref, cos_ref, sin_ref, o_ref, *,
                 rope_mode, rope_cols, scale_lo, scale_hi, scale):
    h = _prenorm(x_ref[...], gain_ref[...], sc_ref[...], sh_ref[...]).astype(BF16)
    cos = cos_ref[...]
    sin = sin_ref[...]
    n = w_ref.shape[1]
    for j in range(n // PROJ_TN):
        lo = j * PROJ_TN
        acc = jnp.dot(h, w_ref[:, lo:lo + PROJ_TN], preferred_element_type=F32)
        if scale_lo <= lo < scale_hi:
            acc = acc * scale
        if lo >= rope_cols:
            o_ref[:, lo:lo + PROJ_TN] = acc.astype(BF16)
        elif rope_mode == "ret":
            for hh in range(PROJ_TN // (2 * LANES)):
                c0 = hh * 2 * LANES
                x1 = acc[:, c0:c0 + LANES]
                x2 = acc[:, c0 + LANES:c0 + 2 * LANES]
                o_ref[:, lo + c0:lo + c0 + LANES] = (x1 * cos - x2 * sin).astype(BF16)
                o_ref[:, lo + c0 + LANES:lo + c0 + 2 * LANES] = (x2 * cos + x1 * sin).astype(BF16)
        else:
            for bb in range(PROJ_TN // LANES):
                c0 = bb * LANES
                xb = acc[:, c0:c0 + LANES]
                o_ref[:, lo + c0:lo + c0 + LANES] = (
                    xb * cos + pltpu.roll(xb, LANES // 2, 1) * sin).astype(BF16)


def _proj(x, gains, gain_idx, mod, sc_idx, sh_idx, w, cos, sin, *, seq, rope_mode,
          rope_cols, scale_lo=0, scale_hi=0, scale=1.0):
    t, d = x.shape
    n = w.shape[1]
    tiles_per_seq = seq // PROJ_TM
    kern = functools.partial(_proj_kernel, rope_mode=rope_mode, rope_cols=rope_cols,
                             scale_lo=scale_lo, scale_hi=scale_hi, scale=scale)
    return pl.pallas_call(
        kern,
        grid=(t // PROJ_TM,),
        in_specs=[
            pl.BlockSpec((PROJ_TM, d), lambda i: (i, 0)),
            pl.BlockSpec((None, 1, d), lambda i: (gain_idx, 0, 0)),
            pl.BlockSpec((None, 1, d), lambda i: (i // tiles_per_seq, 0, sc_idx)),
            pl.BlockSpec((None, 1, d), lambda i: (i // tiles_per_seq, 0, sh_idx)),
            pl.BlockSpec((d, n), lambda i: (0, 0)),
            pl.BlockSpec((PROJ_TM, LANES), lambda i: (i % tiles_per_seq, 0)),
            pl.BlockSpec((PROJ_TM, LANES), lambda i: (i % tiles_per_seq, 0)),
        ],
        out_specs=pl.BlockSpec((PROJ_TM, n), lambda i: (i, 0)),
        out_shape=jax.ShapeDtypeStruct((t, n), BF16),
        compiler_params=_cparams(("arbitrary",)),
        name="proj_" + rope_mode,
    )(x, gains, mod, mod, w, cos, sin)


def _ret_kernel(q_ref, k_ref, v_ref, g_ref, o_ref, state_ref, *, chunk):
    seq = q_ref.shape[0]
    hf = pl.program_id(1).astype(F32)
    log_gamma = jnp.log(1.0 - jnp.exp2(jnp.full((1, 1), -5.0, F32) - hf))
    row = lax.broadcasted_iota(jnp.int32, (chunk, chunk), 0)
    col = lax.broadcasted_iota(jnp.int32, (chunk, chunk), 1)
    diff = (row - col).astype(F32)
    dmask = jnp.where(diff >= 0, jnp.exp(log_gamma * jnp.maximum(diff, 0.0)), 0.0)
    pos = lax.broadcasted_iota(jnp.int32, (chunk, 1), 0).astype(F32)
    q_dec = jnp.exp(log_gamma * (pos + 1.0))
    k_dec = jnp.exp(log_gamma * (chunk - 1.0 - pos))
    c_dec = jnp.exp(log_gamma * float(chunk))

    state_ref[...] = jnp.zeros_like(state_ref)

    def body(n, carry):
        r0 = pl.multiple_of(n * chunk, chunk)
        qc = q_ref[pl.ds(r0, chunk), :]
        kc = k_ref[pl.ds(r0, chunk), :]
        vc = v_ref[pl.ds(r0, chunk), :]
        s = lax.dot_general(qc, kc, (((1,), (1,)), ((), ())), preferred_element_type=F32)
        inner = (s * dmask).astype(BF16)
        qd = (qc.astype(F32) * q_dec).astype(BF16)
        state = state_ref[...]
        out = (jnp.dot(inner, vc, preferred_element_type=F32)
               + jnp.dot(qd, state.astype(BF16), preferred_element_type=F32))
        kd = (kc.astype(F32) * k_dec).astype(BF16)
        state_ref[...] = state * c_dec + lax.dot_general(
            kd, vc, (((0,), (0,)), ((), ())), preferred_element_type=F32)
        gt = g_ref[pl.ds(r0, chunk), :].astype(F32)
        o_ref[pl.ds(r0, chunk), :] = ((gt * _sigmoid(gt)) * _rms(out)).astype(BF16)
        return carry

    lax.fori_loop(0, seq // chunk, body, 0)


def _retention(qkvg, bsz, seq):
    dk = 256
    dv = 512
    qk_blocks = RET_HEADS * dk // dv
    kern = functools.partial(_ret_kernel, chunk=RET_KCHUNK)
    return pl.pallas_call(
        kern,
        grid=(bsz, RET_HEADS),
        in_specs=[
            pl.BlockSpec((None, seq, dk), lambda b, h: (b, 0, h)),
            pl.BlockSpec((None, seq, dk), lambda b, h: (b, 0, RET_HEADS + h)),
            pl.BlockSpec((None, seq, dv), lambda b, h: (b, 0, 2 * qk_blocks + h)),
            pl.BlockSpec((None, seq, dv), lambda b, h: (b, 0, 2 * qk_blocks + RET_HEADS + h)),
        ],
        out_specs=pl.BlockSpec((None, seq, dv), lambda b, h: (b, 0, h)),
        out_shape=jax.ShapeDtypeStruct((bsz, seq, RET_HEADS * dv), BF16),
        scratch_shapes=[pltpu.VMEM((dk, dv), F32)],
        compiler_params=_cparams(("arbitrary", "arbitrary")),
        name="retention",
    )(qkvg, qkvg, qkvg, qkvg)


def _outproj_kernel(y_ref, x_ref, g_ref, w_ref, o_ref):
    o_ref[...] = x_ref[...] + g_ref[...] * jnp.dot(
        y_ref[...], w_ref[...], preferred_element_type=F32)


def _outproj(y, x, mod, g_idx, w, *, seq):
    t, d = x.shape
    kdim = y.shape[1]
    tiles_per_seq = seq // OUT_TM
    return pl.pallas_call(
        _outproj_kernel,
        grid=(t // OUT_TM,),
        in_specs=[
            pl.BlockSpec((OUT_TM, kdim), lambda i: (i, 0)),
            pl.BlockSpec((OUT_TM, d), lambda i: (i, 0)),
            pl.BlockSpec((None, 1, d), lambda i: (i // tiles_per_seq, 0, g_idx)),
            pl.BlockSpec((kdim, d), lambda i: (0, 0)),
        ],
        out_specs=pl.BlockSpec((OUT_TM, d), lambda i: (i, 0)),
        out_shape=jax.ShapeDtypeStruct((t, d), F32),
        compiler_params=_cparams(("arbitrary",)),
        name="outproj",
    )(y, x, mod, w)


def _ffn_kernel(x_ref, gain_ref, sc_ref, sh_ref, g2_ref, win_ref, wconv_ref, bconv_ref,
                wdown_ref, fgain_ref, o_ref, carry_ref, y_ref, acc_ref, *,
                tiles_per_seq, final_norm):
    tm = x_ref.shape[0]
    dff = wdown_ref.shape[0]
    x = x_ref[...]
    h = _prenorm(x, gain_ref[...], sc_ref[...], sh_ref[...]).astype(BF16)
    seq_start = (pl.program_id(0) % tiles_per_seq) == 0
    sqrt_half = math.sqrt(0.5)

    for f in range(dff // FFN_TF):
        lo = f * FFN_TF
        a = jnp.dot(h, win_ref[:, lo:lo + FFN_TF], preferred_element_type=F32)
        g = jnp.dot(h, win_ref[:, dff + lo:dff + lo + FFN_TF], preferred_element_type=F32)
        w0 = wconv_ref[0:1, lo:lo + FFN_TF]
        w1 = wconv_ref[1:2, lo:lo + FFN_TF]
        w2 = wconv_ref[2:3, lo:lo + FFN_TF]
        bias = bconv_ref[:, lo:lo + FFN_TF]

        def conv_gelu_gate(a0, a1, a2, gate):
            ac = w0 * a2 + w1 * a1 + w2 * a0 + bias
            return (0.5 * ac * (1.0 + lax.erf(ac * sqrt_half))) * gate

        y_ref[...] = conv_gelu_gate(a, pltpu.roll(a, 1, 0), pltpu.roll(a, 2, 0), g).astype(BF16)
        prev = jnp.where(seq_start, 0.0, carry_ref[:, lo:lo + FFN_TF])
        ext = jnp.concatenate([prev, a[0:BF16_ROWS]], axis=0)
        top = conv_gelu_gate(a[0:BF16_ROWS],
                             pltpu.roll(ext, 1, 0)[SUBLANES:],
                             pltpu.roll(ext, 2, 0)[SUBLANES:],
                             g[0:BF16_ROWS])
        y_ref[0:BF16_ROWS, :] = top.astype(BF16)
        carry_ref[:, lo:lo + FFN_TF] = a[tm - SUBLANES:tm]
        part = jnp.dot(y_ref[...], wdown_ref[lo:lo + FFN_TF, :], preferred_element_type=F32)
        if f == 0:
            acc_ref[...] = part
        else:
            acc_ref[...] += part

    out = x + g2_ref[...] * acc_ref[...]
    if final_norm:
        out = _rms(out) * fgain_ref[...]
    o_ref[...] = out


def _ffn(x, gains, gain_idx, mod, w_in, w_conv, b_conv, w_down, fgain, *, seq, final_norm):
    t, d = x.shape
    dff = w_down.shape[0]
    tiles_per_seq = seq // FFN_TM
    kern = functools.partial(_ffn_kernel, tiles_per_seq=tiles_per_seq, final_norm=final_norm)
    return pl.pallas_call(
        kern,
        grid=(t // FFN_TM,),
        in_specs=[
            pl.BlockSpec((FFN_TM, d), lambda i: (i, 0)),
            pl.BlockSpec((None, 1, d), lambda i: (gain_idx, 0, 0)),
            pl.BlockSpec((None, 1, d), lambda i: (i // tiles_per_seq, 0, 4)),
            pl.BlockSpec((None, 1, d), lambda i: (i // tiles_per_seq, 0, 3)),
            pl.BlockSpec((None, 1, d), lambda i: (i // tiles_per_seq, 0, 5)),
            pl.BlockSpec((d, 2 * dff), lambda i: (0, 0)),
            pl.BlockSpec((CONV_WIDTH, dff), lambda i: (0, 0)),
            pl.BlockSpec((1, dff), lambda i: (0, 0)),
            pl.BlockSpec((dff, d), lambda i: (0, 0)),
            pl.BlockSpec((1, d), lambda i: (0, 0)),
        ],
        out_specs=pl.BlockSpec((FFN_TM, d), lambda i: (i, 0)),
        out_shape=jax.ShapeDtypeStruct((t, d), F32),
        scratch_shapes=[
            pltpu.VMEM((SUBLANES, dff), F32),
            pltpu.VMEM((FFN_TM, FFN_TF), BF16),
            pltpu.VMEM((FFN_TM, d), F32),
        ],
        compiler_params=_cparams(("arbitrary",)),
        name="convffn",
    )(x, gains, mod, mod, mod, w_in, w_conv, b_conv, w_down, fgain)


def _attn_kernel(lam_ref, q_ref, k_ref, v_ref, gain_ref, o_ref, *, lam_init):
    tq = q_ref.shape[0]
    hd = DIFF_HEAD_DIM
    scale = hd ** -0.5
    qi = pl.program_id(2)

    lv = lam_ref[...]
    lam = (jnp.exp(jnp.sum(lv[0:1] * lv[1:2], axis=1, keepdims=True))
           - jnp.exp(jnp.sum(lv[2:3] * lv[3:4], axis=1, keepdims=True)) + lam_init)

    q = q_ref[...]
    qh = (q[:, :hd], q[:, hd:])
    row = lax.broadcasted_iota(jnp.int32, (tq, tq), 0)
    col = lax.broadcasted_iota(jnp.int32, (tq, tq), 1)
    causal = col <= row

    def step(c, carry, masked):
        r0 = pl.multiple_of(c * tq, tq)
        k = k_ref[pl.ds(r0, tq), :]
        v = v_ref[pl.ds(r0, tq), :]
        new = []
        for half in range(2):
            m, l, acc = carry[half]
            s = lax.dot_general(qh[half], k[:, half * hd:(half + 1) * hd],
                                (((1,), (1,)), ((), ())), preferred_element_type=F32) * scale
            if masked:
                s = jnp.where(causal, s, -jnp.inf)
            m_new = jnp.maximum(m, jnp.max(s, axis=-1, keepdims=True))
            alpha = jnp.exp(m - m_new)
            p = jnp.exp(s - m_new)
            l_new = alpha * l + jnp.sum(p, axis=-1, keepdims=True)
            acc_new = alpha * acc + jnp.dot(p.astype(BF16), v, preferred_element_type=F32)
            new.append((m_new, l_new, acc_new))
        return tuple(new)

    init_half = (jnp.full((tq, 1), -jnp.inf, F32), jnp.zeros((tq, 1), F32),
                 jnp.zeros((tq, DIFF_V_DIM), F32))
    carry = lax.fori_loop(0, qi, lambda c, cr: step(c, cr, False), (init_half, init_half))
    (_, l0, acc0), (_, l1, acc1) = step(qi, carry, True)

    o = acc0 / l0 - lam * (acc1 / l1)
    o_ref[...] = ((_rms(o) * gain_ref[...]) * (1.0 - lam_init)).astype(BF16)


def _attention(q, kv, lam_params, subln_gain, *, lam_init):
    bsz, seq, _ = q.shape
    w = 2 * DIFF_HEAD_DIM
    kern = functools.partial(_attn_kernel, lam_init=lam_init)
    return pl.pallas_call(
        kern,
        grid=(bsz, DIFF_HEADS, seq // ATT_TQ),
        in_specs=[
            pl.BlockSpec((4, DIFF_HEAD_DIM), lambda b, h, i: (0, 0)),
            pl.BlockSpec((None, ATT_TQ, w), lambda b, h, i: (b, i, h)),
            pl.BlockSpec((None, seq, w), lambda b, h, i: (b, 0, h)),
            pl.BlockSpec((None, seq, DIFF_V_DIM), lambda b, h, i: (b, 0, DIFF_HEADS + h)),
            pl.BlockSpec((1, DIFF_V_DIM), lambda b, h, i: (0, 0)),
        ],
        out_specs=pl.BlockSpec((None, ATT_TQ, DIFF_V_DIM), lambda b, h, i: (b, i, h)),
        out_shape=jax.ShapeDtypeStruct((bsz, seq, DIFF_HEADS * DIFF_V_DIM), BF16),
        compiler_params=_cparams(("arbitrary", "arbitrary", "arbitrary")),
        name="diffattn",
    )(lam_params, q, kv, kv, subln_gain)


def kernel(x, c, norm_gain, w_ada, b_ada, ret_w_in, ret_w_out, ffn_w_in, ffn_w_conv,
           ffn_b_conv, ffn_w_down, kv_norm_gain, kv_w_ada, kv_b_ada, w_kv, diff_w_q,
           diff_lambda, diff_subln_gain, diff_w_out, final_norm_gain):
    bsz, seq, d = x.shape
    t = bsz * seq
    depth = w_ada.shape[0]
    n_a = ret_w_in.shape[0]
    assert depth == 2 and n_a == 1 and diff_w_q.shape[0] == 1
    assert seq % PROJ_TM == 0 and seq % FFN_TM == 0 and seq % OUT_TM == 0
    assert seq % ATT_TQ == 0 and seq % RET_KCHUNK == 0

    pos = jnp.arange(seq, dtype=F32)[:, None]
    ret_freqs = 1.0 / (ROPE_THETA ** jnp.linspace(0.0, 1.0, LANES))
    ret_ang = pos * ret_freqs[None, :]
    ret_cos, ret_sin = jnp.cos(ret_ang), jnp.sin(ret_ang)
    diff_freqs = 1.0 / (ROPE_THETA ** (jnp.arange(0, DIFF_HEAD_DIM, 2, dtype=F32) / DIFF_HEAD_DIM))
    diff_ang = pos * diff_freqs[None, :]
    dcos, dsin = jnp.cos(diff_ang), jnp.sin(diff_ang)
    diff_cos = jnp.concatenate([dcos, dcos], axis=1)
    diff_sin = jnp.concatenate([-dsin, dsin], axis=1)

    xf = x.reshape(t, d)
    gains = norm_gain.reshape(depth * 2, 1, d)
    b_ada3 = b_ada.reshape(depth, 1, 6 * d)
    mod0 = _ada(c, w_ada, b_ada3, 0).reshape(bsz, 1, 6 * d)
    mod1 = _ada(c, w_ada, b_ada3, 1).reshape(bsz, 1, 6 * d)
    kvmod = _ada(c, kv_w_ada[None], kv_b_ada.reshape(1, 1, 2 * d), 0).reshape(bsz, 1, 2 * d)

    qk_w = RET_HEADS * 256
    qkvg = _proj(xf, gains, 0, mod0, 1, 0, ret_w_in[0].astype(BF16), ret_cos, ret_sin,
                 seq=seq, rope_mode="ret", rope_cols=2 * qk_w,
                 scale_lo=qk_w, scale_hi=2 * qk_w, scale=256 ** -0.5)
    y = _retention(qkvg.reshape(bsz, seq, -1), bsz, seq).reshape(t, -1)
    xf = _outproj(y, xf, mod0, 2, ret_w_out[0].astype(BF16), seq=seq)
    xf = _ffn(xf, gains, 1, mod0, ffn_w_in[0].astype(BF16), ffn_w_conv[0],
              ffn_b_conv[0][None], ffn_w_down[0].astype(BF16), final_norm_gain[None],
              seq=seq, final_norm=False)

    k_dim = DIFF_HEADS * 2 * DIFF_HEAD_DIM
    kv = _proj(xf, kv_norm_gain.reshape(1, 1, d), 0, kvmod, 1, 0, w_kv.astype(BF16),
               diff_cos, diff_sin, seq=seq, rope_mode="diff", rope_cols=k_dim)

    q = _proj(xf, gains, 2, mod1, 1, 0, diff_w_q[0].astype(BF16), diff_cos, diff_sin,
              seq=seq, rope_mode="diff", rope_cols=k_dim)
    lam_init = 0.8 - 0.6 * math.exp(-0.3 * 1)
    o = _attention(q.reshape(bsz, seq, -1), kv.reshape(bsz, seq, -1), diff_lambda[0],
                   diff_subln_gain[0][None], lam_init=lam_init).reshape(t, -1)
    xf = _outproj(o, xf, mod1, 2, diff_w_out[0].astype(BF16), seq=seq)
    xf = _ffn(xf, gains, 3, mod1, ffn_w_in[1].astype(BF16), ffn_w_conv[1],
              ffn_b_conv[1][None], ffn_w_down[1].astype(BF16), final_norm_gain[None],
              seq=seq, final_norm=True)
    return xf.reshape(bsz, seq, d)
```

```python
import functools
import math

import jax
import jax.numpy as jnp
from jax import lax
from jax.experimental import pallas as pl
from jax.experimental.pallas import tpu as pltpu

F32 = jnp.float32
BF16 = jnp.bfloat16

RET_HEADS = 4
RET_CHUNK = 128
DIFF_HEADS = 4
DIFF_HEAD_DIM = 128
DIFF_V_DIM = 256
CONV_WIDTH = 3
ROPE_THETA = 10000.0
NORM_EPS = 1e-6

LANES = 128
SUBLANES = 8
BF16_ROWS = 16
VMEM_LIMIT_BYTES = 56 * 1024 * 1024

ADA_TN = 512
PROJ_TM = 512
PROJ_TN = 512
RET_KCHUNK = 256
OUT_TM = 512
FFN_TM = 512
FFN_TF = 256
ATT_TQ = 256


def _cparams(semantics):
    return pltpu.CompilerParams(dimension_semantics=semantics,
                                vmem_limit_bytes=VMEM_LIMIT_BYTES)


def _rms(x):
    return x * lax.rsqrt(jnp.mean(x * x, axis=-1, keepdims=True) + NORM_EPS)


def _prenorm(x, gain, sc, sh):
    return (_rms(x) * gain) * (1.0 + sc) + sh


def _sigmoid(x):
    return 1.0 / (1.0 + jnp.exp(-x))


def _ada_kernel(c_ref, w_ref, b_ref, o_ref):
    c = c_ref[...]
    s = (c * _sigmoid(c)).astype(BF16)
    o_ref[...] = jnp.dot(s, w_ref[...].astype(BF16), preferred_element_type=F32) + b_ref[...]


def _ada(c, w, b, layer):
    bsz, d = c.shape
    n = w.shape[-1]
    return pl.pallas_call(
        _ada_kernel,
        grid=(n // ADA_TN,),
        in_specs=[
            pl.BlockSpec((bsz, d), lambda j: (0, 0)),
            pl.BlockSpec((None, d, ADA_TN), lambda j: (layer, 0, j)),
            pl.BlockSpec((None, 1, ADA_TN), lambda j: (layer, 0, j)),
        ],
        out_specs=pl.BlockSpec((bsz, ADA_TN), lambda j: (0, j)),
        out_shape=jax.ShapeDtypeStruct((bsz, n), F32),
        compiler_params=_cparams(("arbitrary",)),
        name="ada",
    )(c, w, b)


def _proj_kernel(x_ref, gain_ref, sc_ref, sh_ref, w_ref, cos_ref, sin_ref, o_ref, *,
                 rope_mode, rope_cols, scale_lo, scale_hi, scale):
    h = _prenorm(x_ref[...], gain_ref[...], sc_ref[...], sh_ref[...]).astype(BF16)
    cos = cos_ref[...]
    sin = sin_ref[...]
    n = w_ref.shape[1]
    for j in range(n // PROJ_TN):
        lo = j * PROJ_TN
        acc = jnp.dot(h, w_ref[:, lo:lo + PROJ_TN], preferred_element_type=F32)
        if scale_lo <= lo < scale_hi:
            acc = acc * scale
        if lo >= rope_cols:
            o_ref[:, lo:lo + PROJ_TN] = acc.astype(BF16)
        elif rope_mode == "ret":
            for hh in range(PROJ_TN // (2 * LANES)):
                c0 = hh * 2 * LANES
                x1 = acc[:, c0:c0 + LANES]
                x2 = acc[:, c0 + LANES:c0 + 2 * LANES]
                o_ref[:, lo + c0:lo + c0 + LANES] = (x1 * cos - x2 * sin).astype(BF16)
                o_ref[:, lo + c0 + LANES:lo + c0 + 2 * LANES] = (x2 * cos + x1 * sin).astype(BF16)
        else:
            for bb in range(PROJ_TN // LANES):
                c0 = bb * LANES
                xb = acc[:, c0:c0 + LANES]
                o_ref[:, lo + c0:lo + c0 + LANES] = (
                    xb * cos + pltpu.roll(xb, LANES // 2, 1) * sin).astype(BF16)


def _proj(x, gains, gain_idx, mod, sc_idx, sh_idx, w, cos, sin, *, seq, rope_mode,
          rope_cols, scale_lo=0, scale_hi=0, scale=1.0):
    t, d = x.shape
    n = w.shape[1]
    tiles_per_seq = seq // PROJ_TM
    kern = functools.partial(_proj_kernel, rope_mode=rope_mode, rope_cols=rope_cols,
                             scale_lo=scale_lo, scale_hi=scale_hi, scale=scale)
    return pl.pallas_call(
        kern,
        grid=(t // PROJ_TM,),
        in_specs=[
            pl.BlockSpec((PROJ_TM, d), lambda i: (i, 0)),
            pl.BlockSpec((None, 1, d), lambda i: (gain_idx, 0, 0)),
            pl.BlockSpec((None, 1, d), lambda i: (i // tiles_per_seq, 0, sc_idx)),
            pl.BlockSpec((None, 1, d), lambda i: (i // tiles_per_seq, 0, sh_idx)),
            pl.BlockSpec((d, n), lambda i: (0, 0)),
            pl.BlockSpec((PROJ_TM, LANES), lambda i: (i % tiles_per_seq, 0)),
            pl.BlockSpec((PROJ_TM, LANES), lambda i: (i % tiles_per_seq, 0)),
        ],
        out_specs=pl.BlockSpec((PROJ_TM, n), lambda i: (i, 0)),
        out_shape=jax.ShapeDtypeStruct((t, n), BF16),
        compiler_params=_cparams(("arbitrary",)),
        name="proj_" + rope_mode,
    )(x, gains, mod, mod, w, cos, sin)


def _ret_kernel(q_ref, k_ref, v_ref, g_ref, o_ref, state_ref, *, chunk):
    seq = q_ref.shape[0]
    hf = pl.program_id(1).astype(F32)
    log_gamma = jnp.log(1.0 - jnp.exp2(jnp.full((1, 1), -5.0, F32) - hf))
    row = lax.broadcasted_iota(jnp.int32, (chunk, chunk), 0)
    col = lax.broadcasted_iota(jnp.int32, (chunk, chunk), 1)
    diff = (row - col).astype(F32)
    dmask = jnp.where(diff >= 0, jnp.exp(log_gamma * jnp.maximum(diff, 0.0)), 0.0)
    pos = lax.broadcasted_iota(jnp.int32, (chunk, 1), 0).astype(F32)
    q_dec = jnp.exp(log_gamma * (pos + 1.0))
    k_dec = jnp.exp(log_gamma * (chunk - 1.0 - pos))
    c_dec = jnp.exp(log_gamma * float(chunk))

    state_ref[...] = jnp.zeros_like(state_ref)

    def body(n, carry):
        r0 = pl.multiple_of(n * chunk, chunk)
        qc = q_ref[pl.ds(r0, chunk), :]
        kc = k_ref[pl.ds(r0, chunk), :]
        vc = v_ref[pl.ds(r0, chunk), :]
        s = lax.dot_general(qc, kc, (((1,), (1,)), ((), ())), preferred_element_type=F32)
        inner = (s * dmask).astype(BF16)
        qd = (qc.astype(F32) * q_dec).astype(BF16)
        state = state_ref[...]
        out = (jnp.dot(inner, vc, preferred_element_type=F32)
               + jnp.dot(qd, state.astype(BF16), preferred_element_type=F32))
        kd = (kc.astype(F32) * k_dec).astype(BF16)
        state_ref[...] = state * c_dec + lax.dot_general(
            kd, vc, (((0,), (0,)), ((), ())), preferred_element_type=F32)
        gt = g_ref[pl.ds(r0, chunk), :].astype(F32)
        o_ref[pl.ds(r0, chunk), :] = ((gt * _sigmoid(gt)) * _rms(out)).astype(BF16)
        return carry

    lax.fori_loop(0, seq // chunk, body, 0)


def _retention(qkvg, bsz, seq):
    dk = 256
    dv = 512
    qk_blocks = RET_HEADS * dk // dv
    kern = functools.partial(_ret_kernel, chunk=RET_KCHUNK)
    return pl.pallas_call(
        kern,
        grid=(bsz, RET_HEADS),
        in_specs=[
            pl.BlockSpec((None, seq, dk), lambda b, h: (b, 0, h)),
            pl.BlockSpec((None, seq, dk), lambda b, h: (b, 0, RET_HEADS + h)),
            pl.BlockSpec((None, seq, dv), lambda b, h: (b, 0, 2 * qk_blocks + h)),
            pl.BlockSpec((None, seq, dv), lambda b, h: (b, 0, 2 * qk_blocks + RET_HEADS + h)),
        ],
        out_specs=pl.BlockSpec((None, seq, dv), lambda b, h: (b, 0, h)),
        out_shape=jax.ShapeDtypeStruct((bsz, seq, RET_HEADS * dv), BF16),
        scratch_shapes=[pltpu.VMEM((dk, dv), F32)],
        compiler_params=_cparams(("arbitrary", "arbitrary")),
        name="retention",
    )(qkvg, qkvg, qkvg, qkvg)


def _outproj_kernel(y_ref, x_ref, g_ref, w_ref, o_ref):
    o_ref[...] = x_ref[...] + g_ref[...] * jnp.dot(
        y_ref[...], w_ref[...], preferred_element_type=F32)


def _outproj(y, x, mod, g_idx, w, *, seq):
    t, d = x.shape
    kdim = y.shape[1]
    tiles_per_seq = seq // OUT_TM
    return pl.pallas_call(
        _outproj_kernel,
        grid=(t // OUT_TM,),
        in_specs=[
            pl.BlockSpec((OUT_TM, kdim), lambda i: (i, 0)),
            pl.BlockSpec((OUT_TM, d), lambda i: (i, 0)),
            pl.BlockSpec((None, 1, d), lambda i: (i // tiles_per_seq, 0, g_idx)),
            pl.BlockSpec((kdim, d), lambda i: (0, 0)),
        ],
        out_specs=pl.BlockSpec((OUT_TM, d), lambda i: (i, 0)),
        out_shape=jax.ShapeDtypeStruct((t, d), F32),
        compiler_params=_cparams(("arbitrary",)),
        name="outproj",
    )(y, x, mod, w)


def _ffn_kernel(x_ref, gain_ref, sc_ref, sh_ref, g2_ref, win_ref, wconv_ref, bconv_ref,
                wdown_ref, fgain_ref, o_ref, carry_ref, y_ref, acc_ref, *,
                tiles_per_seq, final_norm):
    tm = x_ref.shape[0]
    dff = wdown_ref.shape[0]
    x = x_ref[...]
    h = _prenorm(x, gain_ref[...], sc_ref[...], sh_ref[...]).astype(BF16)
    seq_start = (pl.program_id(0) % tiles_per_seq) == 0
    sqrt_half = math.sqrt(0.5)

    for f in range(dff // FFN_TF):
        lo = f * FFN_TF
        a = jnp.dot(h, win_ref[:, lo:lo + FFN_TF], preferred_element_type=F32)
        g = jnp.dot(h, win_ref[:, dff + lo:dff + lo + FFN_TF], preferred_element_type=F32)
        w0 = wconv_ref[0:1, lo:lo + FFN_TF]
        w1 = wconv_ref[1:2, lo:lo + FFN_TF]
        w2 = wconv_ref[2:3, lo:lo + FFN_TF]
        bias = bconv_ref[:, lo:lo + FFN_TF]

        def conv_gelu_gate(a0, a1, a2, gate):
            ac = w0 * a2 + w1 * a1 + w2 * a0 + bias
            return (0.5 * ac * (1.0 + lax.erf(ac * sqrt_half))) * gate

        y_ref[...] = conv_gelu_gate(a, pltpu.roll(a, 1, 0), pltpu.roll(a, 2, 0), g).astype(BF16)
        prev = jnp.where(seq_start, 0.0, carry_ref[:, lo:lo + FFN_TF])
        ext = jnp.concatenate([prev, a[0:BF16_ROWS]], axis=0)
        top = conv_gelu_gate(a[0:BF16_ROWS],
                             pltpu.roll(ext, 1, 0)[SUBLANES:],
                             pltpu.roll(ext, 2, 0)[SUBLANES:],
                             g[0:BF16_ROWS])
        y_ref[0:BF16_ROWS, :] = top.astype(BF16)
        carry_ref[:, lo:lo + FFN_TF] = a[tm - SUBLANES:tm]
        part = jnp.dot(y_ref[...], wdown_ref[lo:lo + FFN_TF, :], preferred_element_type=F32)
        if f == 0:
            acc_ref[...] = part
        else:
            acc_ref[...] += part

    out = x + g2_ref[...] * acc_ref[...]
    if final_norm:
        out = _rms(out) * fgain_ref[...]
    o_ref[...] = out


def _ffn(x, gains, gain_idx, mod, w_in, w_conv, b_conv, w_down, fgain, *, seq, final_norm):
    t, d = x.shape
    dff = w_down.shape[0]
    tiles_per_seq = seq // FFN_TM
    kern = functools.partial(_ffn_kernel, tiles_per_seq=tiles_per_seq, final_norm=final_norm)
    return pl.pallas_call(
        kern,
        grid=(t // FFN_TM,),
        in_specs=[
            pl.BlockSpec((FFN_TM, d), lambda i: (i, 0)),
            pl.BlockSpec((None, 1, d), lambda i: (gain_idx, 0, 0)),
            pl.BlockSpec((None, 1, d), lambda i: (i // tiles_per_seq, 0, 4)),
            pl.BlockSpec((None, 1, d), lambda i: (i // tiles_per_seq, 0, 3)),
            pl.BlockSpec((None, 1, d), lambda i: (i // tiles_per_seq, 0, 5)),
            pl.BlockSpec((d, 2 * dff), lambda i: (0, 0)),
            pl.BlockSpec((CONV_WIDTH, dff), lambda i: (0, 0)),
            pl.BlockSpec((1, dff), lambda i: (0, 0)),
            pl.BlockSpec((dff, d), lambda i: (0, 0)),
            pl.BlockSpec((1, d), lambda i: (0, 0)),
        ],
        out_specs=pl.BlockSpec((FFN_TM, d), lambda i: (i, 0)),
        out_shape=jax.ShapeDtypeStruct((t, d), F32),
        scratch_shapes=[
            pltpu.VMEM((SUBLANES, dff), F32),
            pltpu.VMEM((FFN_TM, FFN_TF), BF16),
            pltpu.VMEM((FFN_TM, d), F32),
        ],
        compiler_params=_cparams(("arbitrary",)),
        name="convffn",
    )(x, gains, mod, mod, mod, w_in, w_conv, b_conv, w_down, fgain)


def _attn_kernel(lam_ref, q_ref, k_ref, v_ref, gain_ref, o_ref, *, lam_init):
    seq = q_ref.shape[0]
    tq = ATT_TQ
    hd = DIFF_HEAD_DIM
    exp_scale = (hd ** -0.5) * math.log2(math.e)
    contract_last = (((1,), (1,)), ((), ()))

    lv = lam_ref[...]
    lam = (jnp.exp(jnp.sum(lv[0:1] * lv[1:2], axis=1, keepdims=True))
           - jnp.exp(jnp.sum(lv[2:3] * lv[3:4], axis=1, keepdims=True)) + lam_init)

    row = lax.broadcasted_iota(jnp.int32, (tq, tq), 0)
    col = lax.broadcasted_iota(jnp.int32, (tq, tq), 1)
    causal = col <= row

    for qi in range(seq // tq):
        r0 = qi * tq
        v_diag = v_ref[r0:r0 + tq, :]
        halves = []
        for half in range(2):
            c0 = half * hd
            qh = q_ref[r0:r0 + tq, c0:c0 + hd]
            s_diag = lax.dot_general(qh, k_ref[r0:r0 + tq, c0:c0 + hd], contract_last,
                                     preferred_element_type=F32)
            s_diag = jnp.where(causal, s_diag, -jnp.inf)
            m = jnp.max(s_diag, axis=-1, keepdims=True)
            if qi > 0:
                s_main = lax.dot_general(qh, k_ref[0:r0, c0:c0 + hd], contract_last,
                                         preferred_element_type=F32)
                m = jnp.maximum(m, jnp.max(s_main, axis=-1, keepdims=True))
            p_diag = jnp.exp2((s_diag - m) * exp_scale)
            l = jnp.sum(p_diag, axis=-1, keepdims=True)
            acc = jnp.dot(p_diag.astype(BF16), v_diag, preferred_element_type=F32)
            if qi > 0:
                p_main = jnp.exp2((s_main - m) * exp_scale)
                l = l + jnp.sum(p_main, axis=-1, keepdims=True)
                acc = acc + jnp.dot(p_main.astype(BF16), v_ref[0:r0, :],
                                    preferred_element_type=F32)
            halves.append(acc * (1.0 / l))
        o = halves[0] - lam * halves[1]
        o_ref[r0:r0 + tq, :] = ((_rms(o) * gain_ref[...]) * (1.0 - lam_init)).astype(BF16)


def _attention(q, kv, lam_params, subln_gain, *, lam_init):
    bsz, seq, _ = q.shape
    w = 2 * DIFF_HEAD_DIM
    kern = functools.partial(_attn_kernel, lam_init=lam_init)
    return pl.pallas_call(
        kern,
        grid=(bsz, DIFF_HEADS),
        in_specs=[
            pl.BlockSpec((4, DIFF_HEAD_DIM), lambda b, h: (0, 0)),
            pl.BlockSpec((None, seq, w), lambda b, h: (b, 0, h)),
            pl.BlockSpec((None, seq, w), lambda b, h: (b, 0, h)),
            pl.BlockSpec((None, seq, DIFF_V_DIM), lambda b, h: (b, 0, DIFF_HEADS + h)),
            pl.BlockSpec((1, DIFF_V_DIM), lambda b, h: (0, 0)),
        ],
        out_specs=pl.BlockSpec((None, seq, DIFF_V_DIM), lambda b, h: (b, 0, h)),
        out_shape=jax.ShapeDtypeStruct((bsz, seq, DIFF_HEADS * DIFF_V_DIM), BF16),
        compiler_params=_cparams(("arbitrary", "arbitrary")),
        name="diffattn",
    )(lam_params, q, kv, kv, subln_gain)


def kernel(x, c, norm_gain, w_ada, b_ada, ret_w_in, ret_w_out, ffn_w_in, ffn_w_conv,
           ffn_b_conv, ffn_w_down, kv_norm_gain, kv_w_ada, kv_b_ada, w_kv, diff_w_q,
           diff_lambda, diff_subln_gain, diff_w_out, final_norm_gain):
    bsz, seq, d = x.shape
    t = bsz * seq
    depth = w_ada.shape[0]
    n_a = ret_w_in.shape[0]
    assert depth == 2 and n_a == 1 and diff_w_q.shape[0] == 1
    assert seq % PROJ_TM == 0 and seq % FFN_TM == 0 and seq % OUT_TM == 0
    assert seq % ATT_TQ == 0 and seq % RET_KCHUNK == 0

    pos = jnp.arange(seq, dtype=F32)[:, None]
    ret_freqs = 1.0 / (ROPE_THETA ** jnp.linspace(0.0, 1.0, LANES))
    ret_ang = pos * ret_freqs[None, :]
    ret_cos, ret_sin = jnp.cos(ret_ang), jnp.sin(ret_ang)
    diff_freqs = 1.0 / (ROPE_THETA ** (jnp.arange(0, DIFF_HEAD_DIM, 2, dtype=F32) / DIFF_HEAD_DIM))
    diff_ang = pos * diff_freqs[None, :]
    dcos, dsin = jnp.cos(diff_ang), jnp.sin(diff_ang)
    diff_cos = jnp.concatenate([dcos, dcos], axis=1)
    diff_sin = jnp.concatenate([-dsin, dsin], axis=1)

    xf = x.reshape(t, d)
    gains = norm_gain.reshape(depth * 2, 1, d)
    b_ada3 = b_ada.reshape(depth, 1, 6 * d)
    mod0 = _ada(c, w_ada, b_ada3, 0).reshape(bsz, 1, 6 * d)
    mod1 = _ada(c, w_ada, b_ada3, 1).reshape(bsz, 1, 6 * d)
    kvmod = _ada(c, kv_w_ada[None], kv_b_ada.reshape(1, 1, 2 * d), 0).reshape(bsz, 1, 2 * d)

    qk_w = RET_HEADS * 256
    qkvg = _proj(xf, gains, 0, mod0, 1, 0, ret_w_in[0].astype(BF16), ret_cos, ret_sin,
                 seq=seq, rope_mode="ret", rope_cols=2 * qk_w,
                 scale_lo=qk_w, scale_hi=2 * qk_w, scale=256 ** -0.5)
    y = _retention(qkvg.reshape(bsz, seq, -1), bsz, seq).reshape(t, -1)
    xf = _outproj(y, xf, mod0, 2, ret_w_out[0].astype(BF16), seq=seq)
    xf = _ffn(xf, gains, 1, mod0, ffn_w_in[0].astype(BF16), ffn_w_conv[0],
              ffn_b_conv[0][None], ffn_w_down[0].astype(BF16), final_norm_gain[None],
              seq=seq, final_norm=False)

    k_dim = DIFF_HEADS * 2 * DIFF_HEAD_DIM
    kv = _proj(xf, kv_norm_gain.reshape(1, 1, d), 0, kvmod, 1, 0, w_kv.astype(BF16),
               diff_cos, diff_sin, seq=seq, rope_mode="diff", rope_cols=k_dim)

    q = _proj(xf, gains, 2, mod1, 1, 0, diff_w_q[0].astype(BF16), diff_cos, diff_sin,
              seq=seq, rope_mode="diff", rope_cols=k_dim)
    lam_init = 0.8 - 0.6 * math.exp(-0.3 * 1)
    o = _attention(q.reshape(bsz, seq, -1), kv.reshape(bsz, seq, -1), diff_lambda[0],
                   diff_subln_gain[0][None], lam_init=lam_init).reshape(t, -1)
    xf = _outproj(o, xf, mod1, 2, diff_w_out[0].astype(BF16), seq=seq)
    xf = _ffn(xf, gains, 3, mod1, ffn_w_in[1].astype(BF16), ffn_w_conv[1],
              ffn_b_conv[1][None], ffn_w_down[1].astype(BF16), final_norm_gain[None],
              seq=seq, final_norm=True)
    return xf.reshape(bsz, seq, d)
```

```python
import functools
import math

import jax
import jax.numpy as jnp
from jax import lax
from jax.experimental import pallas as pl
from jax.experimental.pallas import tpu as pltpu

F32 = jnp.float32
BF16 = jnp.bfloat16

RET_HEADS = 4
RET_QK_DIM = 256
RET_V_DIM = 512
DIFF_HEADS = 4
DIFF_HEAD_DIM = 128
DIFF_V_DIM = 256
CONV_WIDTH = 3
ROPE_THETA = 10000.0
NORM_EPS = 1e-6

LANES = 128
SUBLANES = 8
BF16_ROWS = 16
VMEM_LIMIT_BYTES = 56 * 1024 * 1024

ADA_TN = 512
PROJ_TM = 512
PROJ_TN = 512
RET_KCHUNK = 256
OUT_TM = 512
FFN_TM = 512
FFN_TF = 256
ATT_TQ = 256


def _cparams(semantics):
    return pltpu.CompilerParams(dimension_semantics=semantics,
                                vmem_limit_bytes=VMEM_LIMIT_BYTES)


def _rms(x):
    return x * lax.rsqrt(jnp.mean(x * x, axis=-1, keepdims=True) + NORM_EPS)


def _prenorm(x, gain, sc, sh):
    return (_rms(x) * gain) * (1.0 + sc) + sh


def _sigmoid(x):
    return 1.0 / (1.0 + jnp.exp(-x))


def _ada_kernel(c_ref, w_ref, b_ref, o_ref):
    c = c_ref[...]
    s = (c * _sigmoid(c)).astype(BF16)
    o_ref[...] = jnp.dot(s, w_ref[...].astype(BF16), preferred_element_type=F32) + b_ref[...]


def _ada(c, w, b, layer):
    bsz, d = c.shape
    n = w.shape[-1]
    return pl.pallas_call(
        _ada_kernel,
        grid=(n // ADA_TN,),
        in_specs=[
            pl.BlockSpec((bsz, d), lambda j: (0, 0)),
            pl.BlockSpec((None, d, ADA_TN), lambda j: (layer, 0, j)),
            pl.BlockSpec((None, 1, ADA_TN), lambda j: (layer, 0, j)),
        ],
        out_specs=pl.BlockSpec((bsz, ADA_TN), lambda j: (0, j)),
        out_shape=jax.ShapeDtypeStruct((bsz, n), F32),
        compiler_params=_cparams(("arbitrary",)),
        name="ada",
    )(c, w, b)


def _rope_ret(acc, lo, cos, sin, pos1, o_ref):
    qk_w = RET_HEADS * RET_QK_DIM
    for hh in range(PROJ_TN // RET_QK_DIM):
        c0 = hh * RET_QK_DIM
        col = lo + c0
        head = (col % qk_w) // RET_QK_DIM
        log_gamma = math.log(1.0 - 2.0 ** (-5.0 - head))
        if col < qk_w:
            dec = jnp.exp(pos1 * log_gamma)
        else:
            dec = jnp.exp(pos1 * (-log_gamma)) * (RET_QK_DIM ** -0.5)
        x1 = acc[:, c0:c0 + LANES]
        x2 = acc[:, c0 + LANES:c0 + 2 * LANES]
        o_ref[:, col:col + LANES] = ((x1 * cos - x2 * sin) * dec).astype(BF16)
        o_ref[:, col + LANES:col + 2 * LANES] = ((x2 * cos + x1 * sin) * dec).astype(BF16)


def _rope_diff(acc, lo, cos, sin, o_ref):
    for bb in range(PROJ_TN // LANES):
        c0 = bb * LANES
        xb = acc[:, c0:c0 + LANES]
        o_ref[:, lo + c0:lo + c0 + LANES] = (
            xb * cos + pltpu.roll(xb, LANES // 2, 1) * sin).astype(BF16)


def _proj_kernel(*refs, n_branch, rope_mode, rope_cols):
    x_ref = refs[0]
    cos = refs[1 + 4 * n_branch][...]
    sin = refs[2 + 4 * n_branch][...]
    out_refs = refs[3 + 4 * n_branch:]
    tm = x_ref.shape[0]
    normed = _rms(x_ref[...])
    if rope_mode == "ret":
        row = lax.broadcasted_iota(jnp.int32, (tm, 1), 0)
        pos1 = (jnp.bitwise_and(row, RET_KCHUNK - 1) + 1).astype(F32)
    for br in range(n_branch):
        gain_ref, sc_ref, sh_ref, w_ref = refs[1 + 4 * br:5 + 4 * br]
        o_ref = out_refs[br]
        h = ((normed * gain_ref[...]) * (1.0 + sc_ref[...]) + sh_ref[...]).astype(BF16)
        for j in range(w_ref.shape[1] // PROJ_TN):
            lo = j * PROJ_TN
            acc = jnp.dot(h, w_ref[:, lo:lo + PROJ_TN], preferred_element_type=F32)
            if lo >= rope_cols:
                o_ref[:, lo:lo + PROJ_TN] = acc.astype(BF16)
            elif rope_mode == "ret":
                _rope_ret(acc, lo, cos, sin, pos1, o_ref)
            else:
                _rope_diff(acc, lo, cos, sin, o_ref)


def _proj(x, branches, cos, sin, *, seq, rope_mode, rope_cols):
    t, d = x.shape
    tiles_per_seq = seq // PROJ_TM
    in_specs = [pl.BlockSpec((PROJ_TM, d), lambda i: (i, 0))]
    args = [x]
    out_specs, out_shapes = [], []
    for gains, gain_idx, mod, sc_idx, sh_idx, w in branches:
        n = w.shape[1]
        in_specs += [
            pl.BlockSpec((None, 1, d), lambda i, g=gain_idx: (g, 0, 0)),
            pl.BlockSpec((None, 1, d), lambda i, c=sc_idx: (i // tiles_per_seq, 0, c)),
            pl.BlockSpec((None, 1, d), lambda i, c=sh_idx: (i // tiles_per_seq, 0, c)),
            pl.BlockSpec((d, n), lambda i: (0, 0)),
        ]
        args += [gains, mod, mod, w]
        out_specs.append(pl.BlockSpec((PROJ_TM, n), lambda i: (i, 0)))
        out_shapes.append(jax.ShapeDtypeStruct((t, n), BF16))
    in_specs += [pl.BlockSpec((PROJ_TM, LANES), lambda i: (i % tiles_per_seq, 0))] * 2
    args += [cos, sin]
    kern = functools.partial(_proj_kernel, n_branch=len(branches), rope_mode=rope_mode,
                             rope_cols=rope_cols)
    return pl.pallas_call(
        kern,
        grid=(t // PROJ_TM,),
        in_specs=in_specs,
        out_specs=out_specs,
        out_shape=out_shapes,
        compiler_params=_cparams(("arbitrary",)),
        name="proj_" + rope_mode,
    )(*args)


def _ret_kernel(q_ref, k_ref, v_ref, g_ref, o_ref):
    seq = q_ref.shape[0]
    chunk = RET_KCHUNK
    hf = pl.program_id(1).astype(F32)
    log_gamma = jnp.log(1.0 - jnp.exp2(jnp.full((1, 1), -5.0, F32) - hf))
    c_dec = jnp.exp(log_gamma * float(chunk))
    row = lax.broadcasted_iota(jnp.int32, (chunk, chunk), 0)
    col = lax.broadcasted_iota(jnp.int32, (chunk, chunk), 1)
    causal = col <= row
    n_chunks = seq // chunk

    state = None
    for n in range(n_chunks):
        r0 = n * chunk
        qc = q_ref[r0:r0 + chunk, :]
        kc = k_ref[r0:r0 + chunk, :]
        vc = v_ref[r0:r0 + chunk, :]
        s = lax.dot_general(qc, kc, (((1,), (1,)), ((), ())), preferred_element_type=F32)
        inner = jnp.where(causal, s, 0.0).astype(BF16)
        out = jnp.dot(inner, vc, preferred_element_type=F32)
        if state is not None:
            out = out + jnp.dot(qc, state.astype(BF16), preferred_element_type=F32)
        if n + 1 < n_chunks:
            kv = lax.dot_general(kc, vc, (((0,), (0,)), ((), ())), preferred_element_type=F32)
            state = c_dec * (kv if state is None else state + kv)
        gt = g_ref[r0:r0 + chunk, :].astype(F32)
        o_ref[r0:r0 + chunk, :] = ((gt * _sigmoid(gt)) * _rms(out)).astype(BF16)


def _retention(qkvg, bsz, seq):
    dk = RET_QK_DIM
    dv = RET_V_DIM
    qk_blocks = RET_HEADS * dk // dv
    return pl.pallas_call(
        _ret_kernel,
        grid=(bsz, RET_HEADS),
        in_specs=[
            pl.BlockSpec((None, seq, dk), lambda b, h: (b, 0, h)),
            pl.BlockSpec((None, seq, dk), lambda b, h: (b, 0, RET_HEADS + h)),
            pl.BlockSpec((None, seq, dv), lambda b, h: (b, 0, 2 * qk_blocks + h)),
            pl.BlockSpec((None, seq, dv), lambda b, h: (b, 0, 2 * qk_blocks + RET_HEADS + h)),
        ],
        out_specs=pl.BlockSpec((None, seq, dv), lambda b, h: (b, 0, h)),
        out_shape=jax.ShapeDtypeStruct((bsz, seq, RET_HEADS * dv), BF16),
        compiler_params=_cparams(("arbitrary", "arbitrary")),
        name="retention",
    )(qkvg, qkvg, qkvg, qkvg)


def _outproj_kernel(y_ref, x_ref, g_ref, w_ref, o_ref):
    o_ref[...] = x_ref[...] + g_ref[...] * jnp.dot(
        y_ref[...], w_ref[...], preferred_element_type=F32)


def _outproj(y, x, mod, g_idx, w, *, seq):
    t, d = x.shape
    kdim = y.shape[1]
    tiles_per_seq = seq // OUT_TM
    return pl.pallas_call(
        _outproj_kernel,
        grid=(t // OUT_TM,),
        in_specs=[
            pl.BlockSpec((OUT_TM, kdim), lambda i: (i, 0)),
            pl.BlockSpec((OUT_TM, d), lambda i: (i, 0)),
            pl.BlockSpec((None, 1, d), lambda i: (i // tiles_per_seq, 0, g_idx)),
            pl.BlockSpec((kdim, d), lambda i: (0, 0)),
        ],
        out_specs=pl.BlockSpec((OUT_TM, d), lambda i: (i, 0)),
        out_shape=jax.ShapeDtypeStruct((t, d), F32),
        compiler_params=_cparams(("arbitrary",)),
        name="outproj",
    )(y, x, mod, w)


def _ffn_kernel(x_ref, gain_ref, sc_ref, sh_ref, g2_ref, win_ref, wconv_ref, bconv_ref,
                wdown_ref, fgain_ref, o_ref, carry_ref, y_ref, *,
                tiles_per_seq, final_norm):
    tm = x_ref.shape[0]
    dff = wdown_ref.shape[0]
    x = x_ref[...]
    h = _prenorm(x, gain_ref[...], sc_ref[...], sh_ref[...]).astype(BF16)
    seq_start = (pl.program_id(0) % tiles_per_seq) == 0
    sqrt_half = math.sqrt(0.5)

    for f in range(dff // FFN_TF):
        lo = f * FFN_TF
        a = jnp.dot(h, win_ref[:, lo:lo + FFN_TF], preferred_element_type=F32)
        g = jnp.dot(h, win_ref[:, dff + lo:dff + lo + FFN_TF], preferred_element_type=F32)
        w0 = wconv_ref[0:1, lo:lo + FFN_TF]
        w1 = wconv_ref[1:2, lo:lo + FFN_TF]
        w2 = wconv_ref[2:3, lo:lo + FFN_TF]
        bias = bconv_ref[:, lo:lo + FFN_TF]

        def conv_gelu_gate(a0, a1, a2, gate):
            ac = w0 * a2 + w1 * a1 + w2 * a0 + bias
            return (0.5 * ac * (1.0 + lax.erf(ac * sqrt_half))) * gate

        y_ref[:, lo:lo + FFN_TF] = conv_gelu_gate(
            a, pltpu.roll(a, 1, 0), pltpu.roll(a, 2, 0), g).astype(BF16)
        prev = jnp.where(seq_start, 0.0, carry_ref[:, lo:lo + FFN_TF])
        ext = jnp.concatenate([prev, a[0:BF16_ROWS]], axis=0)
        top = conv_gelu_gate(a[0:BF16_ROWS],
                             pltpu.roll(ext, 1, 0)[SUBLANES:],
                             pltpu.roll(ext, 2, 0)[SUBLANES:],
                             g[0:BF16_ROWS])
        y_ref[0:BF16_ROWS, lo:lo + FFN_TF] = top.astype(BF16)
        carry_ref[:, lo:lo + FFN_TF] = a[tm - SUBLANES:tm]

    out = x + g2_ref[...] * jnp.dot(y_ref[...], wdown_ref[...], preferred_element_type=F32)
    if final_norm:
        out = _rms(out) * fgain_ref[...]
    o_ref[...] = out


def _ffn(x, gains, gain_idx, mod, w_in, w_conv, b_conv, w_down, fgain, *, seq, final_norm):
    t, d = x.shape
    dff = w_down.shape[0]
    tiles_per_seq = seq // FFN_TM
    kern = functools.partial(_ffn_kernel, tiles_per_seq=tiles_per_seq, final_norm=final_norm)
    return pl.pallas_call(
        kern,
        grid=(t // FFN_TM,),
        in_specs=[
            pl.BlockSpec((FFN_TM, d), lambda i: (i, 0)),
            pl.BlockSpec((None, 1, d), lambda i: (gain_idx, 0, 0)),
            pl.BlockSpec((None, 1, d), lambda i: (i // tiles_per_seq, 0, 4)),
            pl.BlockSpec((None, 1, d), lambda i: (i // tiles_per_seq, 0, 3)),
            pl.BlockSpec((None, 1, d), lambda i: (i // tiles_per_seq, 0, 5)),
            pl.BlockSpec((d, 2 * dff), lambda i: (0, 0)),
            pl.BlockSpec((CONV_WIDTH, dff), lambda i: (0, 0)),
            pl.BlockSpec((1, dff), lambda i: (0, 0)),
            pl.BlockSpec((dff, d), lambda i: (0, 0)),
            pl.BlockSpec((1, d), lambda i: (0, 0)),
        ],
        out_specs=pl.BlockSpec((FFN_TM, d), lambda i: (i, 0)),
        out_shape=jax.ShapeDtypeStruct((t, d), F32),
        scratch_shapes=[
            pltpu.VMEM((SUBLANES, dff), F32),
            pltpu.VMEM((FFN_TM, dff), BF16),
        ],
        compiler_params=_cparams(("arbitrary",)),
        name="convffn",
    )(x, gains, mod, mod, mod, w_in, w_conv, b_conv, w_down, fgain)


def _attn_kernel(lam_ref, q_ref, k_ref, v_ref, gain_ref, o_ref, *, lam_init):
    seq = q_ref.shape[0]
    tq = ATT_TQ
    hd = DIFF_HEAD_DIM
    exp_scale = (hd ** -0.5) * math.log2(math.e)
    contract_last = (((1,), (1,)), ((), ()))

    lv = lam_ref[...]
    lam = (jnp.exp(jnp.sum(lv[0:1] * lv[1:2], axis=1, keepdims=True))
           - jnp.exp(jnp.sum(lv[2:3] * lv[3:4], axis=1, keepdims=True)) + lam_init)

    row = lax.broadcasted_iota(jnp.int32, (tq, tq), 0)
    col = lax.broadcasted_iota(jnp.int32, (tq, tq), 1)
    causal = col <= row

    for qi in range(seq // tq):
        r0 = qi * tq
        v_diag = v_ref[r0:r0 + tq, :]
        halves = []
        for half in range(2):
            c0 = half * hd
            qh = q_ref[r0:r0 + tq, c0:c0 + hd]
            s_diag = lax.dot_general(qh, k_ref[r0:r0 + tq, c0:c0 + hd], contract_last,
                                     preferred_element_type=F32)
            s_diag = jnp.where(causal, s_diag, -jnp.inf)
            m = jnp.max(s_diag, axis=-1, keepdims=True)
            if qi > 0:
                s_main = lax.dot_general(qh, k_ref[0:r0, c0:c0 + hd], contract_last,
                                         preferred_element_type=F32)
                m = jnp.maximum(m, jnp.max(s_main, axis=-1, keepdims=True))
            p_diag = jnp.exp2((s_diag - m) * exp_scale)
            l = jnp.sum(p_diag, axis=-1, keepdims=True)
            acc = jnp.dot(p_diag.astype(BF16), v_diag, preferred_element_type=F32)
            if qi > 0:
                p_main = jnp.exp2((s_main - m) * exp_scale)
                l = l + jnp.sum(p_main, axis=-1, keepdims=True)
                acc = acc + jnp.dot(p_main.astype(BF16), v_ref[0:r0, :],
                                    preferred_element_type=F32)
            halves.append(acc * (1.0 / l))
        o = halves[0] - lam * halves[1]
        o_ref[r0:r0 + tq, :] = ((_rms(o) * gain_ref[...]) * (1.0 - lam_init)).astype(BF16)


def _attention(q, kv, lam_params, subln_gain, *, lam_init):
    bsz, seq, _ = q.shape
    w = 2 * DIFF_HEAD_DIM
    kern = functools.partial(_attn_kernel, lam_init=lam_init)
    return pl.pallas_call(
        kern,
        grid=(bsz, DIFF_HEADS),
        in_specs=[
            pl.BlockSpec((4, DIFF_HEAD_DIM), lambda b, h: (0, 0)),
            pl.BlockSpec((None, seq, w), lambda b, h: (b, 0, h)),
            pl.BlockSpec((None, seq, w), lambda b, h: (b, 0, h)),
            pl.BlockSpec((None, seq, DIFF_V_DIM), lambda b, h: (b, 0, DIFF_HEADS + h)),
            pl.BlockSpec((1, DIFF_V_DIM), lambda b, h: (0, 0)),
        ],
        out_specs=pl.BlockSpec((None, seq, DIFF_V_DIM), lambda b, h: (b, 0, h)),
        out_shape=jax.ShapeDtypeStruct((bsz, seq, DIFF_HEADS * DIFF_V_DIM), BF16),
        compiler_params=_cparams(("arbitrary", "arbitrary")),
        name="diffattn",
    )(lam_params, q, kv, kv, subln_gain)


def kernel(x, c, norm_gain, w_ada, b_ada, ret_w_in, ret_w_out, ffn_w_in, ffn_w_conv,
           ffn_b_conv, ffn_w_down, kv_norm_gain, kv_w_ada, kv_b_ada, w_kv, diff_w_q,
           diff_lambda, diff_subln_gain, diff_w_out, final_norm_gain):
    bsz, seq, d = x.shape
    t = bsz * seq
    depth = w_ada.shape[0]
    n_a = ret_w_in.shape[0]
    assert depth == 2 and n_a == 1 and diff_w_q.shape[0] == 1
    assert seq % PROJ_TM == 0 and seq % FFN_TM == 0 and seq % OUT_TM == 0
    assert seq % ATT_TQ == 0 and PROJ_TM % RET_KCHUNK == 0
    assert RET_KCHUNK & (RET_KCHUNK - 1) == 0

    pos = jnp.arange(seq, dtype=F32)[:, None]
    ret_freqs = 1.0 / (ROPE_THETA ** jnp.linspace(0.0, 1.0, LANES))
    ret_ang = pos * ret_freqs[None, :]
    ret_cos, ret_sin = jnp.cos(ret_ang), jnp.sin(ret_ang)
    diff_freqs = 1.0 / (ROPE_THETA ** (jnp.arange(0, DIFF_HEAD_DIM, 2, dtype=F32) / DIFF_HEAD_DIM))
    diff_ang = pos * diff_freqs[None, :]
    dcos, dsin = jnp.cos(diff_ang), jnp.sin(diff_ang)
    diff_cos = jnp.concatenate([dcos, dcos], axis=1)
    diff_sin = jnp.concatenate([-dsin, dsin], axis=1)

    xf = x.reshape(t, d)
    gains = norm_gain.reshape(depth * 2, 1, d)
    b_ada3 = b_ada.reshape(depth, 1, 6 * d)
    mod0 = _ada(c, w_ada, b_ada3, 0).reshape(bsz, 1, 6 * d)
    mod1 = _ada(c, w_ada, b_ada3, 1).reshape(bsz, 1, 6 * d)
    kvmod = _ada(c, kv_w_ada[None], kv_b_ada.reshape(1, 1, 2 * d), 0).reshape(bsz, 1, 2 * d)

    (qkvg,) = _proj(xf, [(gains, 0, mod0, 1, 0, ret_w_in[0].astype(BF16))], ret_cos, ret_sin,
                    seq=seq, rope_mode="ret", rope_cols=2 * RET_HEADS * RET_QK_DIM)
    y = _retention(qkvg.reshape(bsz, seq, -1), bsz, seq).reshape(t, -1)
    xf = _outproj(y, xf, mod0, 2, ret_w_out[0].astype(BF16), seq=seq)
    xf = _ffn(xf, gains, 1, mod0, ffn_w_in[0].astype(BF16), ffn_w_conv[0],
              ffn_b_conv[0][None], ffn_w_down[0].astype(BF16), final_norm_gain[None],
              seq=seq, final_norm=False)

    k_dim = DIFF_HEADS * 2 * DIFF_HEAD_DIM
    kv, q = _proj(xf, [(kv_norm_gain.reshape(1, 1, d), 0, kvmod, 1, 0, w_kv.astype(BF16)),
                       (gains, 2, mod1, 1, 0, diff_w_q[0].astype(BF16))],
                  diff_cos, diff_sin, seq=seq, rope_mode="diff", rope_cols=k_dim)

    lam_init = 0.8 - 0.6 * math.exp(-0.3 * 1)
    o = _attention(q.reshape(bsz, seq, -1), kv.reshape(bsz, seq, -1), diff_lambda[0],
                   diff_subln_gain[0][None], lam_init=lam_init).reshape(t, -1)
    xf = _outproj(o, xf, mod1, 2, diff_w_out[0].astype(BF16), seq=seq)
    xf = _ffn(xf, gains, 3, mod1, ffn_w_in[1].astype(BF16), ffn_w_conv[1],
              ffn_b_conv[1][None], ffn_w_down[1].astype(BF16), final_norm_gain[None],
              seq=seq, final_norm=True)
    return xf.reshape(bsz, seq, d)
```

```python
import functools
import math

import jax
import jax.numpy as jnp
from jax import lax
from jax.experimental import pallas as pl
from jax.experimental.pallas import tpu as pltpu

F32 = jnp.float32
BF16 = jnp.bfloat16

RET_HEADS = 4
RET_QK_DIM = 256
RET_V_DIM = 512
DIFF_HEADS = 4
DIFF_HEAD_DIM = 128
DIFF_V_DIM = 256
CONV_WIDTH = 3
ROPE_THETA = 10000.0
NORM_EPS = 1e-6

LANES = 128
SUBLANES = 8
BF16_ROWS = 16
VMEM_LIMIT_BYTES = 56 * 1024 * 1024

ADA_TN = 512
PROJ_TM = 512
PROJ_TN = 512
RET_KCHUNK = 256
OUT_TM = 512
FFN_TM = 512
FFN_TF = 256
ATT_TQ = 256


def _cparams(semantics):
    return pltpu.CompilerParams(dimension_semantics=semantics,
                                vmem_limit_bytes=VMEM_LIMIT_BYTES)


def _rms(x):
    return x * lax.rsqrt(jnp.mean(x * x, axis=-1, keepdims=True) + NORM_EPS)


def _prenorm(x, gain, sc, sh):
    return (_rms(x) * gain) * (1.0 + sc) + sh


def _sigmoid(x):
    return 1.0 / (1.0 + jnp.exp(-x))


def _resident_weight_spec(w, layer):
    _, k, n = w.shape
    return pl.BlockSpec((None, k, n), lambda *_: (layer, 0, 0), pipeline_mode=pl.Buffered(1))


def _ada_kernel(c_ref, w_ref, b_ref, o_ref):
    c = c_ref[...]
    s = (c * _sigmoid(c)).astype(BF16)
    o_ref[...] = jnp.dot(s, w_ref[...].astype(BF16), preferred_element_type=F32) + b_ref[...]


def _ada(c, w, b, layer):
    bsz, d = c.shape
    n = w.shape[-1]
    return pl.pallas_call(
        _ada_kernel,
        grid=(n // ADA_TN,),
        in_specs=[
            pl.BlockSpec((bsz, d), lambda j: (0, 0)),
            pl.BlockSpec((None, d, ADA_TN), lambda j: (layer, 0, j)),
            pl.BlockSpec((None, 1, ADA_TN), lambda j: (layer, 0, j)),
        ],
        out_specs=pl.BlockSpec((bsz, ADA_TN), lambda j: (0, j)),
        out_shape=jax.ShapeDtypeStruct((bsz, n), F32),
        compiler_params=_cparams(("arbitrary",)),
        name="ada",
    )(c, w, b)


def _rope_ret(acc, lo, cos, sin, pos1, o_ref):
    qk_w = RET_HEADS * RET_QK_DIM
    for hh in range(PROJ_TN // RET_QK_DIM):
        c0 = hh * RET_QK_DIM
        col = lo + c0
        head = (col % qk_w) // RET_QK_DIM
        log_gamma = math.log(1.0 - 2.0 ** (-5.0 - head))
        if col < qk_w:
            dec = jnp.exp(pos1 * log_gamma)
        else:
            dec = jnp.exp(pos1 * (-log_gamma)) * (RET_QK_DIM ** -0.5)
        x1 = acc[:, c0:c0 + LANES]
        x2 = acc[:, c0 + LANES:c0 + 2 * LANES]
        o_ref[:, col:col + LANES] = ((x1 * cos - x2 * sin) * dec).astype(BF16)
        o_ref[:, col + LANES:col + 2 * LANES] = ((x2 * cos + x1 * sin) * dec).astype(BF16)


def _rope_diff(acc, lo, cos, sin, o_ref):
    for bb in range(PROJ_TN // LANES):
        c0 = bb * LANES
        xb = acc[:, c0:c0 + LANES]
        o_ref[:, lo + c0:lo + c0 + LANES] = (
            xb * cos + pltpu.roll(xb, LANES // 2, 1) * sin).astype(BF16)


def _proj_kernel(*refs, n_branch, rope_mode, rope_cols, silu_from):
    x_ref = refs[0]
    cos = refs[1 + 4 * n_branch][...]
    sin = refs[2 + 4 * n_branch][...]
    out_refs = refs[3 + 4 * n_branch:]
    tm = x_ref.shape[0]
    normed = _rms(x_ref[...])
    if rope_mode == "ret":
        row = lax.broadcasted_iota(jnp.int32, (tm, 1), 0)
        pos1 = (jnp.bitwise_and(row, RET_KCHUNK - 1) + 1).astype(F32)
    for br in range(n_branch):
        gain_ref, sc_ref, sh_ref, w_ref = refs[1 + 4 * br:5 + 4 * br]
        o_ref = out_refs[br]
        h = ((normed * gain_ref[...]) * (1.0 + sc_ref[...]) + sh_ref[...]).astype(BF16)
        for j in range(w_ref.shape[1] // PROJ_TN):
            lo = j * PROJ_TN
            acc = jnp.dot(h, w_ref[:, lo:lo + PROJ_TN].astype(BF16),
                          preferred_element_type=F32)
            if lo >= silu_from:
                o_ref[:, lo:lo + PROJ_TN] = (acc * _sigmoid(acc)).astype(BF16)
            elif lo >= rope_cols:
                o_ref[:, lo:lo + PROJ_TN] = acc.astype(BF16)
            elif rope_mode == "ret":
                _rope_ret(acc, lo, cos, sin, pos1, o_ref)
            else:
                _rope_diff(acc, lo, cos, sin, o_ref)


def _proj(x, branches, cos, sin, *, seq, rope_mode, rope_cols, silu_from=None):
    t, d = x.shape
    tiles_per_seq = seq // PROJ_TM
    in_specs = [pl.BlockSpec((PROJ_TM, d), lambda i: (i, 0))]
    args = [x]
    out_specs, out_shapes = [], []
    for gains, gain_idx, mod, sc_idx, sh_idx, w in branches:
        n = w.shape[-1]
        in_specs += [
            pl.BlockSpec((None, 1, d), lambda i, g=gain_idx: (g, 0, 0)),
            pl.BlockSpec((None, 1, d), lambda i, c=sc_idx: (i // tiles_per_seq, 0, c)),
            pl.BlockSpec((None, 1, d), lambda i, c=sh_idx: (i // tiles_per_seq, 0, c)),
            _resident_weight_spec(w, 0),
        ]
        args += [gains, mod, mod, w]
        out_specs.append(pl.BlockSpec((PROJ_TM, n), lambda i: (i, 0)))
        out_shapes.append(jax.ShapeDtypeStruct((t, n), BF16))
    in_specs += [pl.BlockSpec((PROJ_TM, LANES), lambda i: (i % tiles_per_seq, 0))] * 2
    args += [cos, sin]
    if silu_from is None:
        silu_from = max(w.shape[-1] for *_, w in branches)
    kern = functools.partial(_proj_kernel, n_branch=len(branches), rope_mode=rope_mode,
                             rope_cols=rope_cols, silu_from=silu_from)
    return pl.pallas_call(
        kern,
        grid=(t // PROJ_TM,),
        in_specs=in_specs,
        out_specs=out_specs,
        out_shape=out_shapes,
        compiler_params=_cparams(("arbitrary",)),
        name="proj_" + rope_mode,
    )(*args)


def _ret_kernel(q_ref, k_ref, v_ref, g_ref, o_ref):
    seq = q_ref.shape[0]
    chunk = RET_KCHUNK
    hf = pl.program_id(1).astype(F32)
    log_gamma = jnp.log(1.0 - jnp.exp2(jnp.full((1, 1), -5.0, F32) - hf))
    c_dec = jnp.exp(log_gamma * float(chunk))
    row = lax.broadcasted_iota(jnp.int32, (chunk, chunk), 0)
    col = lax.broadcasted_iota(jnp.int32, (chunk, chunk), 1)
    causal = col <= row
    n_chunks = seq // chunk

    state = None
    for n in range(n_chunks):
        r0 = n * chunk
        qc = q_ref[r0:r0 + chunk, :]
        kc = k_ref[r0:r0 + chunk, :]
        vc = v_ref[r0:r0 + chunk, :]
        s = lax.dot_general(qc, kc, (((1,), (1,)), ((), ())), preferred_element_type=F32)
        inner = jnp.where(causal, s, 0.0).astype(BF16)
        out = jnp.dot(inner, vc, preferred_element_type=F32)
        if state is not None:
            out = out + jnp.dot(qc, state.astype(BF16), preferred_element_type=F32)
        if n + 1 < n_chunks:
            kv = lax.dot_general(kc, vc, (((0,), (0,)), ((), ())), preferred_element_type=F32)
            state = c_dec * (kv if state is None else state + kv)
        o_ref[r0:r0 + chunk, :] = (g_ref[r0:r0 + chunk, :].astype(F32) * _rms(out)).astype(BF16)


def _retention(qkvg, bsz, seq):
    dk = RET_QK_DIM
    dv = RET_V_DIM
    qk_blocks = RET_HEADS * dk // dv
    return pl.pallas_call(
        _ret_kernel,
        grid=(bsz, RET_HEADS),
        in_specs=[
            pl.BlockSpec((None, seq, dk), lambda b, h: (b, 0, h)),
            pl.BlockSpec((None, seq, dk), lambda b, h: (b, 0, RET_HEADS + h)),
            pl.BlockSpec((None, seq, dv), lambda b, h: (b, 0, 2 * qk_blocks + h)),
            pl.BlockSpec((None, seq, dv), lambda b, h: (b, 0, 2 * qk_blocks + RET_HEADS + h)),
        ],
        out_specs=pl.BlockSpec((None, seq, dv), lambda b, h: (b, 0, h)),
        out_shape=jax.ShapeDtypeStruct((bsz, seq, RET_HEADS * dv), BF16),
        compiler_params=_cparams(("arbitrary", "arbitrary")),
        name="retention",
    )(qkvg, qkvg, qkvg, qkvg)


def _outproj_kernel(y_ref, x_ref, g_ref, w_ref, o_ref):
    o_ref[...] = x_ref[...] + g_ref[...] * jnp.dot(
        y_ref[...], w_ref[...].astype(BF16), preferred_element_type=F32)


def _outproj(y, x, mod, g_idx, w, layer, *, seq):
    t, d = x.shape
    kdim = y.shape[1]
    tiles_per_seq = seq // OUT_TM
    return pl.pallas_call(
        _outproj_kernel,
        grid=(t // OUT_TM,),
        in_specs=[
            pl.BlockSpec((OUT_TM, kdim), lambda i: (i, 0)),
            pl.BlockSpec((OUT_TM, d), lambda i: (i, 0)),
            pl.BlockSpec((None, 1, d), lambda i: (i // tiles_per_seq, 0, g_idx)),
            _resident_weight_spec(w, layer),
        ],
        out_specs=pl.BlockSpec((OUT_TM, d), lambda i: (i, 0)),
        out_shape=jax.ShapeDtypeStruct((t, d), F32),
        compiler_params=_cparams(("arbitrary",)),
        name="outproj",
    )(y, x, mod, w)


def _ffn_kernel(x_ref, gain_ref, sc_ref, sh_ref, g2_ref, win_ref, wconv_ref, bconv_ref,
                wdown_ref, fgain_ref, o_ref, carry_ref, y_ref, *,
                tiles_per_seq, final_norm):
    tm = x_ref.shape[0]
    dff = wdown_ref.shape[0]
    x = x_ref[...]
    h = _prenorm(x, gain_ref[...], sc_ref[...], sh_ref[...]).astype(BF16)
    seq_start = (pl.program_id(0) % tiles_per_seq) == 0
    sqrt_half = math.sqrt(0.5)

    for f in range(dff // FFN_TF):
        lo = f * FFN_TF
        a = jnp.dot(h, win_ref[:, lo:lo + FFN_TF].astype(BF16), preferred_element_type=F32)
        g = jnp.dot(h, win_ref[:, dff + lo:dff + lo + FFN_TF].astype(BF16),
                    preferred_element_type=F32)
        w0 = wconv_ref[0:1, lo:lo + FFN_TF]
        w1 = wconv_ref[1:2, lo:lo + FFN_TF]
        w2 = wconv_ref[2:3, lo:lo + FFN_TF]
        bias = bconv_ref[:, lo:lo + FFN_TF]

        def conv_gelu_gate(a0, a1, a2, gate):
            ac = w0 * a2 + w1 * a1 + w2 * a0 + bias
            return (0.5 * ac * (1.0 + lax.erf(ac * sqrt_half))) * gate

        y_ref[:, lo:lo + FFN_TF] = conv_gelu_gate(
            a, pltpu.roll(a, 1, 0), pltpu.roll(a, 2, 0), g).astype(BF16)
        prev = jnp.where(seq_start, 0.0, carry_ref[:, lo:lo + FFN_TF])
        ext = jnp.concatenate([prev, a[0:BF16_ROWS]], axis=0)
        top = conv_gelu_gate(a[0:BF16_ROWS],
                             pltpu.roll(ext, 1, 0)[SUBLANES:],
                             pltpu.roll(ext, 2, 0)[SUBLANES:],
                             g[0:BF16_ROWS])
        y_ref[0:BF16_ROWS, lo:lo + FFN_TF] = top.astype(BF16)
        carry_ref[:, lo:lo + FFN_TF] = a[tm - SUBLANES:tm]

    out = x + g2_ref[...] * jnp.dot(y_ref[...], wdown_ref[...].astype(BF16),
                                    preferred_element_type=F32)
    if final_norm:
        out = _rms(out) * fgain_ref[...]
    o_ref[...] = out


def _ffn(x, gains, gain_idx, mod, w_in, w_conv, b_conv, w_down, fgain, layer, *, seq,
         final_norm):
    t, d = x.shape
    dff = w_down.shape[1]
    tiles_per_seq = seq // FFN_TM
    kern = functools.partial(_ffn_kernel, tiles_per_seq=tiles_per_seq, final_norm=final_norm)
    return pl.pallas_call(
        kern,
        grid=(t // FFN_TM,),
        in_specs=[
            pl.BlockSpec((FFN_TM, d), lambda i: (i, 0)),
            pl.BlockSpec((None, 1, d), lambda i: (gain_idx, 0, 0)),
            pl.BlockSpec((None, 1, d), lambda i: (i // tiles_per_seq, 0, 4)),
            pl.BlockSpec((None, 1, d), lambda i: (i // tiles_per_seq, 0, 3)),
            pl.BlockSpec((None, 1, d), lambda i: (i // tiles_per_seq, 0, 5)),
            _resident_weight_spec(w_in, layer),
            pl.BlockSpec((None, CONV_WIDTH, dff), lambda i: (layer, 0, 0)),
            pl.BlockSpec((None, 1, dff), lambda i: (layer, 0, 0)),
            _resident_weight_spec(w_down, layer),
            pl.BlockSpec((1, d), lambda i: (0, 0)),
        ],
        out_specs=pl.BlockSpec((FFN_TM, d), lambda i: (i, 0)),
        out_shape=jax.ShapeDtypeStruct((t, d), F32),
        scratch_shapes=[
            pltpu.VMEM((SUBLANES, dff), F32),
            pltpu.VMEM((FFN_TM, dff), BF16),
        ],
        compiler_params=_cparams(("arbitrary",)),
        name="convffn",
    )(x, gains, mod, mod, mod, w_in, w_conv, b_conv, w_down, fgain)


def _attn_kernel(lam_ref, q_ref, k_ref, v_ref, gain_ref, o_ref, *, lam_init):
    seq = q_ref.shape[0]
    tq = ATT_TQ
    hd = DIFF_HEAD_DIM
    exp_scale = (hd ** -0.5) * math.log2(math.e)
    contract_last = (((1,), (1,)), ((), ()))

    lv = lam_ref[...]
    lam = (jnp.exp(jnp.sum(lv[0:1] * lv[1:2], axis=1, keepdims=True))
           - jnp.exp(jnp.sum(lv[2:3] * lv[3:4], axis=1, keepdims=True)) + lam_init)

    row = lax.broadcasted_iota(jnp.int32, (tq, tq), 0)
    col = lax.broadcasted_iota(jnp.int32, (tq, tq), 1)
    causal = col <= row

    for qi in range(seq // tq):
        r0 = qi * tq
        v_diag = v_ref[r0:r0 + tq, :]
        halves = []
        for half in range(2):
            c0 = half * hd
            qh = q_ref[r0:r0 + tq, c0:c0 + hd]
            s_diag = lax.dot_general(qh, k_ref[r0:r0 + tq, c0:c0 + hd], contract_last,
                                     preferred_element_type=F32)
            s_diag = jnp.where(causal, s_diag, -jnp.inf)
            m = jnp.max(s_diag, axis=-1, keepdims=True)
            if qi > 0:
                s_main = lax.dot_general(qh, k_ref[0:r0, c0:c0 + hd], contract_last,
                                         preferred_element_type=F32)
                m = jnp.maximum(m, jnp.max(s_main, axis=-1, keepdims=True))
            p_diag = jnp.exp2((s_diag - m) * exp_scale)
            l = jnp.sum(p_diag, axis=-1, keepdims=True)
            acc = jnp.dot(p_diag.astype(BF16), v_diag, preferred_element_type=F32)
            if qi > 0:
                p_main = jnp.exp2((s_main - m) * exp_scale)
                l = l + jnp.sum(p_main, axis=-1, keepdims=True)
                acc = acc + jnp.dot(p_main.astype(BF16), v_ref[0:r0, :],
                                    preferred_element_type=F32)
            halves.append(acc * (1.0 / l))
        o = halves[0] - lam * halves[1]
        o_ref[r0:r0 + tq, :] = ((_rms(o) * gain_ref[...]) * (1.0 - lam_init)).astype(BF16)


def _attention(q, kv, lam_params, subln_gain, *, lam_init):
    bsz, seq, _ = q.shape
    w = 2 * DIFF_HEAD_DIM
    kern = functools.partial(_attn_kernel, lam_init=lam_init)
    return pl.pallas_call(
        kern,
        grid=(bsz, DIFF_HEADS),
        in_specs=[
            pl.BlockSpec((4, DIFF_HEAD_DIM), lambda b, h: (0, 0)),
            pl.BlockSpec((None, seq, w), lambda b, h: (b, 0, h)),
            pl.BlockSpec((None, seq, w), lambda b, h: (b, 0, h)),
            pl.BlockSpec((None, seq, DIFF_V_DIM), lambda b, h: (b, 0, DIFF_HEADS + h)),
            pl.BlockSpec((1, DIFF_V_DIM), lambda b, h: (0, 0)),
        ],
        out_specs=pl.BlockSpec((None, seq, DIFF_V_DIM), lambda b, h: (b, 0, h)),
        out_shape=jax.ShapeDtypeStruct((bsz, seq, DIFF_HEADS * DIFF_V_DIM), BF16),
        compiler_params=_cparams(("arbitrary", "arbitrary")),
        name="diffattn",
    )(lam_params, q, kv, kv, subln_gain)


def kernel(x, c, norm_gain, w_ada, b_ada, ret_w_in, ret_w_out, ffn_w_in, ffn_w_conv,
           ffn_b_conv, ffn_w_down, kv_norm_gain, kv_w_ada, kv_b_ada, w_kv, diff_w_q,
           diff_lambda, diff_subln_gain, diff_w_out, final_norm_gain):
    bsz, seq, d = x.shape
    t = bsz * seq
    depth = w_ada.shape[0]
    n_a = ret_w_in.shape[0]
    assert depth == 2 and n_a == 1 and diff_w_q.shape[0] == 1
    assert seq % PROJ_TM == 0 and seq % FFN_TM == 0 and seq % OUT_TM == 0
    assert seq % ATT_TQ == 0 and PROJ_TM % RET_KCHUNK == 0
    assert RET_KCHUNK & (RET_KCHUNK - 1) == 0

    pos = jnp.arange(seq, dtype=F32)[:, None]
    ret_freqs = 1.0 / (ROPE_THETA ** jnp.linspace(0.0, 1.0, LANES))
    ret_ang = pos * ret_freqs[None, :]
    ret_cos, ret_sin = jnp.cos(ret_ang), jnp.sin(ret_ang)
    diff_freqs = 1.0 / (ROPE_THETA ** (jnp.arange(0, DIFF_HEAD_DIM, 2, dtype=F32) / DIFF_HEAD_DIM))
    diff_ang = pos * diff_freqs[None, :]
    dcos, dsin = jnp.cos(diff_ang), jnp.sin(diff_ang)
    diff_cos = jnp.concatenate([dcos, dcos], axis=1)
    diff_sin = jnp.concatenate([-dsin, dsin], axis=1)

    xf = x.reshape(t, d)
    gains = norm_gain.reshape(depth * 2, 1, d)
    b_ada3 = b_ada.reshape(depth, 1, 6 * d)
    ffn_b3 = ffn_b_conv.reshape(depth, 1, -1)
    mod0 = _ada(c, w_ada, b_ada3, 0).reshape(bsz, 1, 6 * d)
    mod1 = _ada(c, w_ada, b_ada3, 1).reshape(bsz, 1, 6 * d)
    kvmod = _ada(c, kv_w_ada[None], kv_b_ada.reshape(1, 1, 2 * d), 0).reshape(bsz, 1, 2 * d)

    qk_cols = 2 * RET_HEADS * RET_QK_DIM
    (qkvg,) = _proj(xf, [(gains, 0, mod0, 1, 0, ret_w_in)], ret_cos, ret_sin, seq=seq,
                    rope_mode="ret", rope_cols=qk_cols,
                    silu_from=qk_cols + RET_HEADS * RET_V_DIM)
    y = _retention(qkvg.reshape(bsz, seq, -1), bsz, seq).reshape(t, -1)
    xf = _outproj(y, xf, mod0, 2, ret_w_out, 0, seq=seq)
    xf = _ffn(xf, gains, 1, mod0, ffn_w_in, ffn_w_conv, ffn_b3, ffn_w_down,
              final_norm_gain[None], 0, seq=seq, final_norm=False)

    k_dim = DIFF_HEADS * 2 * DIFF_HEAD_DIM
    kv, q = _proj(xf, [(kv_norm_gain.reshape(1, 1, d), 0, kvmod, 1, 0, w_kv[None]),
                       (gains, 2, mod1, 1, 0, diff_w_q)],
                  diff_cos, diff_sin, seq=seq, rope_mode="diff", rope_cols=k_dim)

    lam_init = 0.8 - 0.6 * math.exp(-0.3 * 1)
    o = _attention(q.reshape(bsz, seq, -1), kv.reshape(bsz, seq, -1), diff_lambda[0],
                   diff_subln_gain[0][None], lam_init=lam_init).reshape(t, -1)
    xf = _outproj(o, xf, mod1, 2, diff_w_out, 0, seq=seq)
    xf = _ffn(xf, gains, 3, mod1, ffn_w_in, ffn_w_conv, ffn_b3, ffn_w_down,
              final_norm_gain[None], 1, seq=seq, final_norm=True)
    return xf.reshape(bsz, seq, d)
```

```python
import functools
import math

import jax
import jax.numpy as jnp
from jax import lax
from jax.experimental import pallas as pl
from jax.experimental.pallas import tpu as pltpu

F32 = jnp.float32
BF16 = jnp.bfloat16

RET_HEADS = 4
RET_QK_DIM = 256
RET_V_DIM = 512
DIFF_HEADS = 4
DIFF_HEAD_DIM = 128
DIFF_V_DIM = 256
CONV_WIDTH = 3
ROPE_THETA = 10000.0
NORM_EPS = 1e-6

LANES = 128
SUBLANES = 8
BF16_ROWS = 16
VMEM_LIMIT_BYTES = 56 * 1024 * 1024

ADA_TN = 1024
PROJ_TM = 512
PROJ_TN = 512
RET_KCHUNK = 256
OUT_TM = 1024
FFN_TM = 512
FFN_TF = 256
ATT_TQ = 256


def _cparams(semantics):
    return pltpu.CompilerParams(dimension_semantics=semantics,
                                vmem_limit_bytes=VMEM_LIMIT_BYTES)


def _rms(x):
    return x * lax.rsqrt(jnp.mean(x * x, axis=-1, keepdims=True) + NORM_EPS)


def _prenorm(x, gain, sc, sh):
    return (_rms(x) * gain) * (1.0 + sc) + sh


def _sigmoid(x):
    return 1.0 / (1.0 + jnp.exp(-x))


def _resident_weight_spec(w, layer):
    _, k, n = w.shape
    return pl.BlockSpec((None, k, n), lambda *_: (layer, 0, 0), pipeline_mode=pl.Buffered(1))


def _ada_kernel(c_ref, w_ref, b_ref, o_ref):
    c = c_ref[...]
    s = (c * _sigmoid(c)).astype(BF16)
    o_ref[...] = jnp.dot(s, w_ref[...].astype(BF16), preferred_element_type=F32) + b_ref[...]


def _ada(c, w, b, layer):
    bsz, d = c.shape
    n = w.shape[-1]
    return pl.pallas_call(
        _ada_kernel,
        grid=(n // ADA_TN,),
        in_specs=[
            pl.BlockSpec((bsz, d), lambda j: (0, 0)),
            pl.BlockSpec((None, d, ADA_TN), lambda j: (layer, 0, j)),
            pl.BlockSpec((None, 1, ADA_TN), lambda j: (layer, 0, j)),
        ],
        out_specs=pl.BlockSpec((bsz, ADA_TN), lambda j: (0, j)),
        out_shape=jax.ShapeDtypeStruct((bsz, n), F32),
        compiler_params=_cparams(("arbitrary",)),
        name="ada",
    )(c, w, b)


def _rope_ret(acc, lo, cos, sin, pos1, o_ref):
    qk_w = RET_HEADS * RET_QK_DIM
    for hh in range(PROJ_TN // RET_QK_DIM):
        c0 = hh * RET_QK_DIM
        col = lo + c0
        head = (col % qk_w) // RET_QK_DIM
        log_gamma = math.log(1.0 - 2.0 ** (-5.0 - head))
        if col < qk_w:
            dec = jnp.exp(pos1 * log_gamma)
        else:
            dec = jnp.exp(pos1 * (-log_gamma)) * (RET_QK_DIM ** -0.5)
        x1 = acc[:, c0:c0 + LANES]
        x2 = acc[:, c0 + LANES:c0 + 2 * LANES]
        o_ref[:, col:col + LANES] = ((x1 * cos - x2 * sin) * dec).astype(BF16)
        o_ref[:, col + LANES:col + 2 * LANES] = ((x2 * cos + x1 * sin) * dec).astype(BF16)


def _rope_diff(acc, lo, cos, sin, o_ref):
    for bb in range(PROJ_TN // LANES):
        c0 = bb * LANES
        xb = acc[:, c0:c0 + LANES]
        o_ref[:, lo + c0:lo + c0 + LANES] = (
            xb * cos + pltpu.roll(xb, LANES // 2, 1) * sin).astype(BF16)


def _proj_kernel(*refs, n_branch, rope_mode, rope_cols, silu_from):
    x_ref = refs[0]
    cos = refs[1 + 4 * n_branch][...]
    sin = refs[2 + 4 * n_branch][...]
    out_refs = refs[3 + 4 * n_branch:]
    tm = x_ref.shape[0]
    normed = _rms(x_ref[...])
    if rope_mode == "ret":
        row = lax.broadcasted_iota(jnp.int32, (tm, 1), 0)
        pos1 = (jnp.bitwise_and(row, RET_KCHUNK - 1) + 1).astype(F32)
    for br in range(n_branch):
        gain_ref, sc_ref, sh_ref, w_ref = refs[1 + 4 * br:5 + 4 * br]
        o_ref = out_refs[br]
        h = ((normed * gain_ref[...]) * (1.0 + sc_ref[...]) + sh_ref[...]).astype(BF16)
        for j in range(w_ref.shape[1] // PROJ_TN):
            lo = j * PROJ_TN
            acc = jnp.dot(h, w_ref[:, lo:lo + PROJ_TN].astype(BF16),
                          preferred_element_type=F32)
            if lo >= silu_from:
                o_ref[:, lo:lo + PROJ_TN] = (acc * _sigmoid(acc)).astype(BF16)
            elif lo >= rope_cols:
                o_ref[:, lo:lo + PROJ_TN] = acc.astype(BF16)
            elif rope_mode == "ret":
                _rope_ret(acc, lo, cos, sin, pos1, o_ref)
            else:
                _rope_diff(acc, lo, cos, sin, o_ref)


def _proj(x, branches, cos, sin, *, seq, rope_mode, rope_cols, silu_from=None):
    t, d = x.shape
    tiles_per_seq = seq // PROJ_TM
    in_specs = [pl.BlockSpec((PROJ_TM, d), lambda i: (i, 0))]
    args = [x]
    out_specs, out_shapes = [], []
    for gains, gain_idx, mod, sc_idx, sh_idx, w in branches:
        n = w.shape[-1]
        in_specs += [
            pl.BlockSpec((None, 1, d), lambda i, g=gain_idx: (g, 0, 0)),
            pl.BlockSpec((None, 1, d), lambda i, c=sc_idx: (i // tiles_per_seq, 0, c)),
            pl.BlockSpec((None, 1, d), lambda i, c=sh_idx: (i // tiles_per_seq, 0, c)),
            _resident_weight_spec(w, 0),
        ]
        args += [gains, mod, mod, w]
        out_specs.append(pl.BlockSpec((PROJ_TM, n), lambda i: (i, 0)))
        out_shapes.append(jax.ShapeDtypeStruct((t, n), BF16))
    in_specs += [pl.BlockSpec((PROJ_TM, LANES), lambda i: (i % tiles_per_seq, 0))] * 2
    args += [cos, sin]
    if silu_from is None:
        silu_from = max(w.shape[-1] for *_, w in branches)
    kern = functools.partial(_proj_kernel, n_branch=len(branches), rope_mode=rope_mode,
                             rope_cols=rope_cols, silu_from=silu_from)
    return pl.pallas_call(
        kern,
        grid=(t // PROJ_TM,),
        in_specs=in_specs,
        out_specs=out_specs,
        out_shape=out_shapes,
        compiler_params=_cparams(("arbitrary",)),
        name="proj_" + rope_mode,
    )(*args)


def _ret_kernel(q_ref, k_ref, v_ref, g_ref, o_ref):
    seq = q_ref.shape[0]
    chunk = RET_KCHUNK
    hf = pl.program_id(1).astype(F32)
    log_gamma = jnp.log(1.0 - jnp.exp2(jnp.full((1, 1), -5.0, F32) - hf))
    c_dec = jnp.exp(log_gamma * float(chunk))
    row = lax.broadcasted_iota(jnp.int32, (chunk, chunk), 0)
    col = lax.broadcasted_iota(jnp.int32, (chunk, chunk), 1)
    causal = col <= row
    n_chunks = seq // chunk

    def intra(n):
        r0 = n * chunk
        qc = q_ref[r0:r0 + chunk, :]
        kc = k_ref[r0:r0 + chunk, :]
        vc = v_ref[r0:r0 + chunk, :]
        s = lax.dot_general(qc, kc, (((1,), (1,)), ((), ())), preferred_element_type=F32)
        inner = jnp.where(causal, s, 0.0).astype(BF16)
        out = jnp.dot(inner, vc, preferred_element_type=F32)
        kv = None
        if n + 1 < n_chunks:
            kv = lax.dot_general(kc, vc, (((0,), (0,)), ((), ())), preferred_element_type=F32)
        return out, kv

    state = None
    nxt = intra(0)
    for n in range(n_chunks):
        r0 = n * chunk
        out, kv = nxt
        if n + 1 < n_chunks:
            nxt = intra(n + 1)
        if state is not None:
            out = out + jnp.dot(q_ref[r0:r0 + chunk, :], state.astype(BF16),
                                preferred_element_type=F32)
        if kv is not None:
            state = c_dec * (kv if state is None else state + kv)
        o_ref[r0:r0 + chunk, :] = (g_ref[r0:r0 + chunk, :].astype(F32) * _rms(out)).astype(BF16)


def _retention(qkvg, bsz, seq):
    dk = RET_QK_DIM
    dv = RET_V_DIM
    qk_blocks = RET_HEADS * dk // dv
    return pl.pallas_call(
        _ret_kernel,
        grid=(bsz, RET_HEADS),
        in_specs=[
            pl.BlockSpec((None, seq, dk), lambda b, h: (b, 0, h)),
            pl.BlockSpec((None, seq, dk), lambda b, h: (b, 0, RET_HEADS + h)),
            pl.BlockSpec((None, seq, dv), lambda b, h: (b, 0, 2 * qk_blocks + h)),
            pl.BlockSpec((None, seq, dv), lambda b, h: (b, 0, 2 * qk_blocks + RET_HEADS + h)),
        ],
        out_specs=pl.BlockSpec((None, seq, dv), lambda b, h: (b, 0, h)),
        out_shape=jax.ShapeDtypeStruct((bsz, seq, RET_HEADS * dv), BF16),
        compiler_params=_cparams(("arbitrary", "arbitrary")),
        name="retention",
    )(qkvg, qkvg, qkvg, qkvg)


def _outproj_kernel(y_ref, x_ref, g_ref, w_ref, o_ref):
    o_ref[...] = x_ref[...] + g_ref[...] * jnp.dot(
        y_ref[...], w_ref[...].astype(BF16), preferred_element_type=F32)


def _outproj(y, x, mod, g_idx, w, layer, *, seq):
    t, d = x.shape
    kdim = y.shape[1]
    tiles_per_seq = seq // OUT_TM
    return pl.pallas_call(
        _outproj_kernel,
        grid=(t // OUT_TM,),
        in_specs=[
            pl.BlockSpec((OUT_TM, kdim), lambda i: (i, 0)),
            pl.BlockSpec((OUT_TM, d), lambda i: (i, 0)),
            pl.BlockSpec((None, 1, d), lambda i: (i // tiles_per_seq, 0, g_idx)),
            _resident_weight_spec(w, layer),
        ],
        out_specs=pl.BlockSpec((OUT_TM, d), lambda i: (i, 0)),
        out_shape=jax.ShapeDtypeStruct((t, d), F32),
        compiler_params=_cparams(("arbitrary",)),
        name="outproj",
    )(y, x, mod, w)


def _ffn_kernel(x_ref, gain_ref, sc_ref, sh_ref, g2_ref, win_ref, wconv_ref, bconv_ref,
                wdown_ref, fgain_ref, o_ref, carry_ref, y_ref, *,
                tiles_per_seq, final_norm):
    tm = x_ref.shape[0]
    dff = wdown_ref.shape[0]
    x = x_ref[...]
    h = _prenorm(x, gain_ref[...], sc_ref[...], sh_ref[...]).astype(BF16)
    seq_start = (pl.program_id(0) % tiles_per_seq) == 0
    sqrt_half = math.sqrt(0.5)

    for f in range(dff // FFN_TF):
        lo = f * FFN_TF
        a = jnp.dot(h, win_ref[:, lo:lo + FFN_TF].astype(BF16), preferred_element_type=F32)
        g = jnp.dot(h, win_ref[:, dff + lo:dff + lo + FFN_TF].astype(BF16),
                    preferred_element_type=F32)
        w0 = wconv_ref[0:1, lo:lo + FFN_TF]
        w1 = wconv_ref[1:2, lo:lo + FFN_TF]
        w2 = wconv_ref[2:3, lo:lo + FFN_TF]
        bias = bconv_ref[:, lo:lo + FFN_TF]

        def conv_gelu_gate(a0, a1, a2, gate):
            ac = w0 * a2 + w1 * a1 + w2 * a0 + bias
            return (0.5 * ac * (1.0 + lax.erf(ac * sqrt_half))) * gate

        y_ref[:, lo:lo + FFN_TF] = conv_gelu_gate(
            a, pltpu.roll(a, 1, 0), pltpu.roll(a, 2, 0), g).astype(BF16)
        prev = jnp.where(seq_start, 0.0, carry_ref[:, lo:lo + FFN_TF])
        ext = jnp.concatenate([prev, a[0:BF16_ROWS]], axis=0)
        top = conv_gelu_gate(a[0:BF16_ROWS],
                             pltpu.roll(ext, 1, 0)[SUBLANES:],
                             pltpu.roll(ext, 2, 0)[SUBLANES:],
                             g[0:BF16_ROWS])
        y_ref[0:BF16_ROWS, lo:lo + FFN_TF] = top.astype(BF16)
        carry_ref[:, lo:lo + FFN_TF] = a[tm - SUBLANES:tm]

    out = x + g2_ref[...] * jnp.dot(y_ref[...], wdown_ref[...].astype(BF16),
                                    preferred_element_type=F32)
    if final_norm:
        out = _rms(out) * fgain_ref[...]
    o_ref[...] = out


def _ffn(x, gains, gain_idx, mod, w_in, w_conv, b_conv, w_down, fgain, layer, *, seq,
         final_norm):
    t, d = x.shape
    dff = w_down.shape[1]
    tiles_per_seq = seq // FFN_TM
    kern = functools.partial(_ffn_kernel, tiles_per_seq=tiles_per_seq, final_norm=final_norm)
    return pl.pallas_call(
        kern,
        grid=(t // FFN_TM,),
        in_specs=[
            pl.BlockSpec((FFN_TM, d), lambda i: (i, 0)),
            pl.BlockSpec((None, 1, d), lambda i: (gain_idx, 0, 0)),
            pl.BlockSpec((None, 1, d), lambda i: (i // tiles_per_seq, 0, 4)),
            pl.BlockSpec((None, 1, d), lambda i: (i // tiles_per_seq, 0, 3)),
            pl.BlockSpec((None, 1, d), lambda i: (i // tiles_per_seq, 0, 5)),
            _resident_weight_spec(w_in, layer),
            pl.BlockSpec((None, CONV_WIDTH, dff), lambda i: (layer, 0, 0)),
            pl.BlockSpec((None, 1, dff), lambda i: (layer, 0, 0)),
            _resident_weight_spec(w_down, layer),
            pl.BlockSpec((1, d), lambda i: (0, 0)),
        ],
        out_specs=pl.BlockSpec((FFN_TM, d), lambda i: (i, 0)),
        out_shape=jax.ShapeDtypeStruct((t, d), F32),
        scratch_shapes=[
            pltpu.VMEM((SUBLANES, dff), F32),
            pltpu.VMEM((FFN_TM, dff), BF16),
        ],
        compiler_params=_cparams(("arbitrary",)),
        name="convffn",
    )(x, gains, mod, mod, mod, w_in, w_conv, b_conv, w_down, fgain)


def _attn_kernel(lam_ref, q_ref, k_ref, v_ref, gain_ref, o_ref, *, lam_init):
    seq = q_ref.shape[0]
    tq = ATT_TQ
    hd = DIFF_HEAD_DIM
    exp_scale = (hd ** -0.5) * math.log2(math.e)
    contract_last = (((1,), (1,)), ((), ()))

    lv = lam_ref[...]
    lam = (jnp.exp(jnp.sum(lv[0:1] * lv[1:2], axis=1, keepdims=True))
           - jnp.exp(jnp.sum(lv[2:3] * lv[3:4], axis=1, keepdims=True)) + lam_init)

    row = lax.broadcasted_iota(jnp.int32, (tq, tq), 0)
    col = lax.broadcasted_iota(jnp.int32, (tq, tq), 1)
    causal = col <= row

    items = [(qi, half) for qi in reversed(range(seq // tq)) for half in range(2)]
    scores, probs, outs = {}, {}, {}

    def stage_scores(w):
        qi, half = items[w]
        r0, c0 = qi * tq, half * hd
        qh = q_ref[r0:r0 + tq, c0:c0 + hd]
        s_diag = lax.dot_general(qh, k_ref[r0:r0 + tq, c0:c0 + hd], contract_last,
                                 preferred_element_type=F32)
        s_diag = jnp.where(causal, s_diag, -jnp.inf)
        m = jnp.max(s_diag, axis=-1, keepdims=True)
        s_main = None
        if qi > 0:
            s_main = lax.dot_general(qh, k_ref[0:r0, c0:c0 + hd], contract_last,
                                     preferred_element_type=F32)
            m = jnp.maximum(m, jnp.max(s_main, axis=-1, keepdims=True))
        scores[w] = (s_diag, s_main, m)

    def stage_probs(w):
        s_diag, s_main, m = scores.pop(w)
        p_diag = jnp.exp2((s_diag - m) * exp_scale)
        l = jnp.sum(p_diag, axis=-1, keepdims=True)
        p_main = None
        if s_main is not None:
            p_main = jnp.exp2((s_main - m) * exp_scale)
            l = l + jnp.sum(p_main, axis=-1, keepdims=True)
            p_main = p_main.astype(BF16)
        probs[w] = (p_diag.astype(BF16), p_main, l)

    def stage_values(w):
        qi, half = items[w]
        r0 = qi * tq
        p_diag, p_main, l = probs.pop(w)
        acc = jnp.dot(p_diag, v_ref[r0:r0 + tq, :], preferred_element_type=F32)
        if p_main is not None:
            acc = acc + jnp.dot(p_main, v_ref[0:r0, :], preferred_element_type=F32)
        outs[w] = acc * (1.0 / l)
        if half == 1:
            o = outs.pop(w - 1) - lam * outs.pop(w)
            o_ref[r0:r0 + tq, :] = ((_rms(o) * gain_ref[...]) * (1.0 - lam_init)).astype(BF16)

    n_q = len(items) // 2
    for t in range(n_q + 2):
        for half in range(2):
            if t < n_q:
                stage_scores(2 * t + half)
        for half in range(2):
            if 0 <= t - 1 < n_q:
                stage_probs(2 * (t - 1) + half)
        for half in range(2):
            if 0 <= t - 2 < n_q:
                stage_values(2 * (t - 2) + half)


def _attention(q, kv, lam_params, subln_gain, *, lam_init):
    bsz, seq, _ = q.shape
    w = 2 * DIFF_HEAD_DIM
    kern = functools.partial(_attn_kernel, lam_init=lam_init)
    return pl.pallas_call(
        kern,
        grid=(bsz, DIFF_HEADS),
        in_specs=[
            pl.BlockSpec((4, DIFF_HEAD_DIM), lambda b, h: (0, 0)),
            pl.BlockSpec((None, seq, w), lambda b, h: (b, 0, h)),
            pl.BlockSpec((None, seq, w), lambda b, h: (b, 0, h)),
            pl.BlockSpec((None, seq, DIFF_V_DIM), lambda b, h: (b, 0, DIFF_HEADS + h)),
            pl.BlockSpec((1, DIFF_V_DIM), lambda b, h: (0, 0)),
        ],
        out_specs=pl.BlockSpec((None, seq, DIFF_V_DIM), lambda b, h: (b, 0, h)),
        out_shape=jax.ShapeDtypeStruct((bsz, seq, DIFF_HEADS * DIFF_V_DIM), BF16),
        compiler_params=_cparams(("arbitrary", "arbitrary")),
        name="diffattn",
    )(lam_params, q, kv, kv, subln_gain)


def kernel(x, c, norm_gain, w_ada, b_ada, ret_w_in, ret_w_out, ffn_w_in, ffn_w_conv,
           ffn_b_conv, ffn_w_down, kv_norm_gain, kv_w_ada, kv_b_ada, w_kv, diff_w_q,
           diff_lambda, diff_subln_gain, diff_w_out, final_norm_gain):
    bsz, seq, d = x.shape
    t = bsz * seq
    depth = w_ada.shape[0]
    n_a = ret_w_in.shape[0]
    assert depth == 2 and n_a == 1 and diff_w_q.shape[0] == 1
    assert seq % PROJ_TM == 0 and seq % FFN_TM == 0 and seq % OUT_TM == 0
    assert seq % ATT_TQ == 0 and PROJ_TM % RET_KCHUNK == 0
    assert RET_KCHUNK & (RET_KCHUNK - 1) == 0

    pos = jnp.arange(seq, dtype=F32)[:, None]
    ret_freqs = 1.0 / (ROPE_THETA ** jnp.linspace(0.0, 1.0, LANES))
    ret_ang = pos * ret_freqs[None, :]
    ret_cos, ret_sin = jnp.cos(ret_ang), jnp.sin(ret_ang)
    diff_freqs = 1.0 / (ROPE_THETA ** (jnp.arange(0, DIFF_HEAD_DIM, 2, dtype=F32) / DIFF_HEAD_DIM))
    diff_ang = pos * diff_freqs[None, :]
    dcos, dsin = jnp.cos(diff_ang), jnp.sin(diff_ang)
    diff_cos = jnp.concatenate([dcos, dcos], axis=1)
    diff_sin = jnp.concatenate([-dsin, dsin], axis=1)

    xf = x.reshape(t, d)
    gains = norm_gain.reshape(depth * 2, 1, d)
    b_ada3 = b_ada.reshape(depth, 1, 6 * d)
    ffn_b3 = ffn_b_conv.reshape(depth, 1, -1)
    mod0 = _ada(c, w_ada, b_ada3, 0).reshape(bsz, 1, 6 * d)
    mod1 = _ada(c, w_ada, b_ada3, 1).reshape(bsz, 1, 6 * d)
    kvmod = _ada(c, kv_w_ada[None], kv_b_ada.reshape(1, 1, 2 * d), 0).reshape(bsz, 1, 2 * d)

    qk_cols = 2 * RET_HEADS * RET_QK_DIM
    (qkvg,) = _proj(xf, [(gains, 0, mod0, 1, 0, ret_w_in)], ret_cos, ret_sin, seq=seq,
                    rope_mode="ret", rope_cols=qk_cols,
                    silu_from=qk_cols + RET_HEADS * RET_V_DIM)
    y = _retention(qkvg.reshape(bsz, seq, -1), bsz, seq).reshape(t, -1)
    xf = _outproj(y, xf, mod0, 2, ret_w_out, 0, seq=seq)
    xf = _ffn(xf, gains, 1, mod0, ffn_w_in, ffn_w_conv, ffn_b3, ffn_w_down,
              final_norm_gain[None], 0, seq=seq, final_norm=False)

    k_dim = DIFF_HEADS * 2 * DIFF_HEAD_DIM
    kv, q = _proj(xf, [(kv_norm_gain.reshape(1, 1, d), 0, kvmod, 1, 0, w_kv[None]),
                       (gains, 2, mod1, 1, 0, diff_w_q)],
                  diff_cos, diff_sin, seq=seq, rope_mode="diff", rope_cols=k_dim)

    lam_init = 0.8 - 0.6 * math.exp(-0.3 * 1)
    o = _attention(q.reshape(bsz, seq, -1), kv.reshape(bsz, seq, -1), diff_lambda[0],
                   diff_subln_gain[0][None], lam_init=lam_init).reshape(t, -1)
    xf = _outproj(o, xf, mod1, 2, diff_w_out, 0, seq=seq)
    xf = _ffn(xf, gains, 3, mod1, ffn_w_in, ffn_w_conv, ffn_b3, ffn_w_down,
              final_norm_gain[None], 1, seq=seq, final_norm=True)
    return xf.reshape(bsz, seq, d)
```

```python
import functools
import math

import jax
import jax.numpy as jnp
from jax import lax
from jax.experimental import pallas as pl
from jax.experimental.pallas import tpu as pltpu

F32 = jnp.float32
BF16 = jnp.bfloat16

RET_HEADS = 4
RET_QK_DIM = 256
RET_V_DIM = 512
DIFF_HEADS = 4
DIFF_HEAD_DIM = 128
DIFF_V_DIM = 256
CONV_WIDTH = 3
ROPE_THETA = 10000.0
NORM_EPS = 1e-6

LANES = 128
SUBLANES = 8
BF16_ROWS = 16
VMEM_LIMIT_BYTES = 56 * 1024 * 1024

ADA_TN = 1024
PROJ_TM = 512
PROJ_TN = 512
RET_KCHUNK = 256
RET_HEADS_PER_STEP = 2
FFN_TM = 512
FFN_TF = 256
ATT_TQ = 256
WEIGHT_STAGE_BYTES = 3 * 512 * 1024


def _cparams(semantics):
    return pltpu.CompilerParams(dimension_semantics=semantics,
                                vmem_limit_bytes=VMEM_LIMIT_BYTES)


def _rms(x):
    return x * lax.rsqrt(jnp.mean(x * x, axis=-1, keepdims=True) + NORM_EPS)


def _prenorm(x, gain, sc, sh):
    return (_rms(x) * gain) * (1.0 + sc) + sh


def _sigmoid(x):
    return 1.0 / (1.0 + jnp.exp(-x))


def _hbm_weight_spec():
    return pl.BlockSpec(memory_space=pl.ANY)


def _staged_weight_scratch(w):
    _, k, n = w.shape
    rows = max(r for r in range(BF16_ROWS, k + 1, BF16_ROWS)
               if k % r == 0 and r * n * 4 <= WEIGHT_STAGE_BYTES)
    return [pltpu.VMEM((k, n), BF16), pltpu.VMEM((2, rows, n), F32),
            pltpu.SemaphoreType.DMA((2,))]


def _stage_weight_bf16(w_hbm, layer, dst_ref, stage_ref, sem):
    rows = stage_ref.shape[1]
    n_chunks = dst_ref.shape[0] // rows

    def chunk_copy(c):
        return pltpu.make_async_copy(w_hbm.at[layer, pl.ds(c * rows, rows), :],
                                     stage_ref.at[c % 2], sem.at[c % 2])

    chunk_copy(0).start()
    for c in range(n_chunks):
        if c + 1 < n_chunks:
            chunk_copy(c + 1).start()
        chunk_copy(c).wait()
        dst_ref[c * rows:(c + 1) * rows, :] = stage_ref[c % 2].astype(BF16)


def _ada_kernel(c_ref, w_ref, b_ref, o_ref):
    c = c_ref[...]
    s = (c * _sigmoid(c)).astype(BF16)
    o_ref[...] = jnp.dot(s, w_ref[...].astype(BF16), preferred_element_type=F32) + b_ref[...]


def _ada(c, w, b, layer):
    bsz, d = c.shape
    n = w.shape[-1]
    return pl.pallas_call(
        _ada_kernel,
        grid=(n // ADA_TN,),
        in_specs=[
            pl.BlockSpec((bsz, d), lambda j: (0, 0)),
            pl.BlockSpec((None, d, ADA_TN), lambda j: (layer, 0, j)),
            pl.BlockSpec((None, 1, ADA_TN), lambda j: (layer, 0, j)),
        ],
        out_specs=pl.BlockSpec((bsz, ADA_TN), lambda j: (0, j)),
        out_shape=jax.ShapeDtypeStruct((bsz, n), F32),
        compiler_params=_cparams(("arbitrary",)),
        name="ada",
    )(c, w, b)


def _rope_ret(acc, lo, cos, sin, pos1, o_ref):
    qk_w = RET_HEADS * RET_QK_DIM
    for hh in range(PROJ_TN // RET_QK_DIM):
        c0 = hh * RET_QK_DIM
        col = lo + c0
        head = (col % qk_w) // RET_QK_DIM
        log_gamma = math.log(1.0 - 2.0 ** (-5.0 - head))
        if col < qk_w:
            dec = jnp.exp(pos1 * log_gamma)
        else:
            dec = jnp.exp(pos1 * (-log_gamma)) * (RET_QK_DIM ** -0.5)
        x1 = acc[:, c0:c0 + LANES]
        x2 = acc[:, c0 + LANES:c0 + 2 * LANES]
        o_ref[:, col:col + LANES] = ((x1 * cos - x2 * sin) * dec).astype(BF16)
        o_ref[:, col + LANES:col + 2 * LANES] = ((x2 * cos + x1 * sin) * dec).astype(BF16)


def _rope_diff(acc, lo, cos, sin, o_ref):
    for bb in range(PROJ_TN // LANES):
        c0 = bb * LANES
        xb = acc[:, c0:c0 + LANES]
        o_ref[:, lo + c0:lo + c0 + LANES] = (
            xb * cos + pltpu.roll(xb, LANES // 2, 1) * sin).astype(BF16)


def _proj_kernel(*refs, n_branch, rope_mode, rope_cols, silu_from):
    x_ref = refs[0]
    cos = refs[1 + 4 * n_branch][...]
    sin = refs[2 + 4 * n_branch][...]
    out_refs = refs[3 + 4 * n_branch:3 + 5 * n_branch]
    scratch = refs[3 + 5 * n_branch:]
    tm = x_ref.shape[0]

    @pl.when(pl.program_id(0) == 0)
    def _():
        for br in range(n_branch):
            _stage_weight_bf16(refs[4 + 4 * br], 0, *scratch[3 * br:3 * br + 3])

    normed = _rms(x_ref[...])
    if rope_mode == "ret":
        row = lax.broadcasted_iota(jnp.int32, (tm, 1), 0)
        pos1 = (jnp.bitwise_and(row, RET_KCHUNK - 1) + 1).astype(F32)
    for br in range(n_branch):
        gain_ref, sc_ref, sh_ref = refs[1 + 4 * br:4 + 4 * br]
        w_ref = scratch[3 * br]
        o_ref = out_refs[br]
        h = ((normed * gain_ref[...]) * (1.0 + sc_ref[...]) + sh_ref[...]).astype(BF16)
        for j in range(w_ref.shape[1] // PROJ_TN):
            lo = j * PROJ_TN
            acc = jnp.dot(h, w_ref[:, lo:lo + PROJ_TN], preferred_element_type=F32)
            if lo >= silu_from:
                o_ref[:, lo:lo + PROJ_TN] = (acc * _sigmoid(acc)).astype(BF16)
            elif lo >= rope_cols:
                o_ref[:, lo:lo + PROJ_TN] = acc.astype(BF16)
            elif rope_mode == "ret":
                _rope_ret(acc, lo, cos, sin, pos1, o_ref)
            else:
                _rope_diff(acc, lo, cos, sin, o_ref)


def _proj(x, branches, cos, sin, *, seq, rope_mode, rope_cols, silu_from=None):
    t, d = x.shape
    tiles_per_seq = seq // PROJ_TM
    in_specs = [pl.BlockSpec((PROJ_TM, d), lambda i: (i, 0))]
    args = [x]
    out_specs, out_shapes, scratch = [], [], []
    for gains, gain_idx, mod, sc_idx, sh_idx, w in branches:
        n = w.shape[-1]
        in_specs += [
            pl.BlockSpec((None, 1, d), lambda i, g=gain_idx: (g, 0, 0)),
            pl.BlockSpec((None, 1, d), lambda i, c=sc_idx: (i // tiles_per_seq, 0, c)),
            pl.BlockSpec((None, 1, d), lambda i, c=sh_idx: (i // tiles_per_seq, 0, c)),
            _hbm_weight_spec(),
        ]
        args += [gains, mod, mod, w]
        scratch += _staged_weight_scratch(w)
        out_specs.append(pl.BlockSpec((PROJ_TM, n), lambda i: (i, 0)))
        out_shapes.append(jax.ShapeDtypeStruct((t, n), BF16))
    in_specs += [pl.BlockSpec((PROJ_TM, LANES), lambda i: (i % tiles_per_seq, 0))] * 2
    args += [cos, sin]
    if silu_from is None:
        silu_from = max(w.shape[-1] for *_, w in branches)
    kern = functools.partial(_proj_kernel, n_branch=len(branches), rope_mode=rope_mode,
                             rope_cols=rope_cols, silu_from=silu_from)
    return pl.pallas_call(
        kern,
        grid=(t // PROJ_TM,),
        in_specs=in_specs,
        out_specs=out_specs,
        out_shape=out_shapes,
        scratch_shapes=scratch,
        compiler_params=_cparams(("arbitrary",)),
        name="proj_" + rope_mode,
    )(*args)


def _ret_kernel(q_ref, k_ref, v_ref, g_ref, o_ref):
    seq = q_ref.shape[0]
    chunk = RET_KCHUNK
    dk, dv = RET_QK_DIM, RET_V_DIM
    row = lax.broadcasted_iota(jnp.int32, (chunk, chunk), 0)
    col = lax.broadcasted_iota(jnp.int32, (chunk, chunk), 1)
    causal = col <= row
    n_chunks = seq // chunk

    for j in range(RET_HEADS_PER_STEP):
        hf = (pl.program_id(1) * RET_HEADS_PER_STEP + j).astype(F32)
        log_gamma = jnp.log(1.0 - jnp.exp2(jnp.full((1, 1), -5.0, F32) - hf))
        c_dec = jnp.exp(log_gamma * float(chunk))

        def intra(n):
            r0 = n * chunk
            qc = q_ref[r0:r0 + chunk, j * dk:(j + 1) * dk]
            kc = k_ref[r0:r0 + chunk, j * dk:(j + 1) * dk]
            vc = v_ref[r0:r0 + chunk, j * dv:(j + 1) * dv]
            s = lax.dot_general(qc, kc, (((1,), (1,)), ((), ())), preferred_element_type=F32)
            inner = jnp.where(causal, s, 0.0).astype(BF16)
            out = jnp.dot(inner, vc, preferred_element_type=F32)
            kv = None
            if n + 1 < n_chunks:
                kv = lax.dot_general(kc, vc, (((0,), (0,)), ((), ())),
                                     preferred_element_type=F32)
            return out, kv

        state = None
        nxt = intra(0)
        for n in range(n_chunks):
            r0 = n * chunk
            out, kv = nxt
            if n + 1 < n_chunks:
                nxt = intra(n + 1)
            if state is not None:
                out = out + jnp.dot(q_ref[r0:r0 + chunk, j * dk:(j + 1) * dk],
                                    state.astype(BF16), preferred_element_type=F32)
            if kv is not None:
                state = c_dec * (kv if state is None else state + kv)
            gate = g_ref[r0:r0 + chunk, j * dv:(j + 1) * dv].astype(F32)
            o_ref[r0:r0 + chunk, j * dv:(j + 1) * dv] = (gate * _rms(out)).astype(BF16)


def _retention(qkvg, bsz, seq):
    hps = RET_HEADS_PER_STEP
    dk = hps * RET_QK_DIM
    dv = hps * RET_V_DIM
    groups = RET_HEADS // hps
    qk_blocks = RET_HEADS * RET_QK_DIM // dv
    return pl.pallas_call(
        _ret_kernel,
        grid=(bsz, groups),
        in_specs=[
            pl.BlockSpec((None, seq, dk), lambda b, h: (b, 0, h)),
            pl.BlockSpec((None, seq, dk), lambda b, h: (b, 0, groups + h)),
            pl.BlockSpec((None, seq, dv), lambda b, h: (b, 0, 2 * qk_blocks + h)),
            pl.BlockSpec((None, seq, dv), lambda b, h: (b, 0, 2 * qk_blocks + groups + h)),
        ],
        out_specs=pl.BlockSpec((None, seq, dv), lambda b, h: (b, 0, h)),
        out_shape=jax.ShapeDtypeStruct((bsz, seq, RET_HEADS * RET_V_DIM), BF16),
        compiler_params=_cparams(("arbitrary", "arbitrary")),
        name="retention",
    )(qkvg, qkvg, qkvg, qkvg)


def _ffn_kernel(mix_ref, x_ref, g1_ref, gain_ref, sc_ref, sh_ref, g2_ref, wout_hbm, win_hbm,
                wconv_ref, bconv_ref, wdown_hbm, fgain_ref, o_ref, carry_ref, y_ref,
                wout_ref, wout_stage, wout_sem, win_ref, win_stage, win_sem,
                wdown_ref, wdown_stage, wdown_sem, *, layer, tiles_per_seq, final_norm):
    tm = x_ref.shape[0]
    dff = wdown_ref.shape[0]

    @pl.when(pl.program_id(0) == 0)
    def _():
        _stage_weight_bf16(wout_hbm, 0, wout_ref, wout_stage, wout_sem)
        _stage_weight_bf16(win_hbm, layer, win_ref, win_stage, win_sem)
        _stage_weight_bf16(wdown_hbm, layer, wdown_ref, wdown_stage, wdown_sem)

    x = x_ref[...] + g1_ref[...] * jnp.dot(mix_ref[...], wout_ref[...],
                                           preferred_element_type=F32)
    h = _prenorm(x, gain_ref[...], sc_ref[...], sh_ref[...]).astype(BF16)
    seq_start = (pl.program_id(0) % tiles_per_seq) == 0
    sqrt_half = math.sqrt(0.5)

    for f in range(dff // FFN_TF):
        lo = f * FFN_TF
        a = jnp.dot(h, win_ref[:, lo:lo + FFN_TF], preferred_element_type=F32)
        g = jnp.dot(h, win_ref[:, dff + lo:dff + lo + FFN_TF], preferred_element_type=F32)
        w0 = wconv_ref[0:1, lo:lo + FFN_TF]
        w1 = wconv_ref[1:2, lo:lo + FFN_TF]
        w2 = wconv_ref[2:3, lo:lo + FFN_TF]
        bias = bconv_ref[:, lo:lo + FFN_TF]

        def conv_gelu_gate(a0, a1, a2, gate):
            ac = w0 * a2 + w1 * a1 + w2 * a0 + bias
            return (0.5 * ac * (1.0 + lax.erf(ac * sqrt_half))) * gate

        y_ref[:, lo:lo + FFN_TF] = conv_gelu_gate(
            a, pltpu.roll(a, 1, 0), pltpu.roll(a, 2, 0), g).astype(BF16)
        prev = jnp.where(seq_start, 0.0, carry_ref[:, lo:lo + FFN_TF])
        ext = jnp.concatenate([prev, a[0:BF16_ROWS]], axis=0)
        top = conv_gelu_gate(a[0:BF16_ROWS],
                             pltpu.roll(ext, 1, 0)[SUBLANES:],
                             pltpu.roll(ext, 2, 0)[SUBLANES:],
                             g[0:BF16_ROWS])
        y_ref[0:BF16_ROWS, lo:lo + FFN_TF] = top.astype(BF16)
        carry_ref[:, lo:lo + FFN_TF] = a[tm - SUBLANES:tm]

    out = x + g2_ref[...] * jnp.dot(y_ref[...], wdown_ref[...], preferred_element_type=F32)
    if final_norm:
        out = _rms(out) * fgain_ref[...]
    o_ref[...] = out


def _ffn(mix, x, gains, gain_idx, mod, w_out, w_in, w_conv, b_conv, w_down, fgain, layer, *,
         seq, final_norm):
    t, d = x.shape
    kmix = mix.shape[1]
    dff = w_down.shape[1]
    tiles_per_seq = seq // FFN_TM
    kern = functools.partial(_ffn_kernel, layer=layer, tiles_per_seq=tiles_per_seq,
                             final_norm=final_norm)
    return pl.pallas_call(
        kern,
        grid=(t // FFN_TM,),
        in_specs=[
            pl.BlockSpec((FFN_TM, kmix), lambda i: (i, 0)),
            pl.BlockSpec((FFN_TM, d), lambda i: (i, 0)),
            pl.BlockSpec((None, 1, d), lambda i: (i // tiles_per_seq, 0, 2)),
            pl.BlockSpec((None, 1, d), lambda i: (gain_idx, 0, 0)),
            pl.BlockSpec((None, 1, d), lambda i: (i // tiles_per_seq, 0, 4)),
            pl.BlockSpec((None, 1, d), lambda i: (i // tiles_per_seq, 0, 3)),
            pl.BlockSpec((None, 1, d), lambda i: (i // tiles_per_seq, 0, 5)),
            _hbm_weight_spec(),
            _hbm_weight_spec(),
            pl.BlockSpec((None, CONV_WIDTH, dff), lambda i: (layer, 0, 0)),
            pl.BlockSpec((None, 1, dff), lambda i: (layer, 0, 0)),
            _hbm_weight_spec(),
            pl.BlockSpec((1, d), lambda i: (0, 0)),
        ],
        out_specs=pl.BlockSpec((FFN_TM, d), lambda i: (i, 0)),
        out_shape=jax.ShapeDtypeStruct((t, d), F32),
        scratch_shapes=[
            pltpu.VMEM((SUBLANES, dff), F32),
            pltpu.VMEM((FFN_TM, dff), BF16),
        ] + (_staged_weight_scratch(w_out) + _staged_weight_scratch(w_in)
             + _staged_weight_scratch(w_down)),
        compiler_params=_cparams(("arbitrary",)),
        name="convffn",
    )(mix, x, mod, gains, mod, mod, mod, w_out, w_in, w_conv, b_conv, w_down, fgain)


def _attn_kernel(lam_ref, q_ref, k_ref, v_ref, gain_ref, o_ref, *, lam_init):
    seq = q_ref.shape[0]
    tq = ATT_TQ
    hd = DIFF_HEAD_DIM
    exp_scale = (hd ** -0.5) * math.log2(math.e)
    contract_last = (((1,), (1,)), ((), ()))

    lv = lam_ref[...]
    lam = (jnp.exp(jnp.sum(lv[0:1] * lv[1:2], axis=1, keepdims=True))
           - jnp.exp(jnp.sum(lv[2:3] * lv[3:4], axis=1, keepdims=True)) + lam_init)

    row = lax.broadcasted_iota(jnp.int32, (tq, tq), 0)
    col = lax.broadcasted_iota(jnp.int32, (tq, tq), 1)
    causal = col <= row

    items = [(qi, half) for qi in reversed(range(seq // tq)) for half in range(2)]
    scores, probs, outs = {}, {}, {}

    def stage_scores(w):
        qi, half = items[w]
        r0, c0 = qi * tq, half * hd
        qh = q_ref[r0:r0 + tq, c0:c0 + hd]
        s_diag = lax.dot_general(qh, k_ref[r0:r0 + tq, c0:c0 + hd], contract_last,
                                 preferred_element_type=F32)
        s_diag = jnp.where(causal, s_diag, -jnp.inf)
        m = jnp.max(s_diag, axis=-1, keepdims=True)
        s_main = None
        if qi > 0:
            s_main = lax.dot_general(qh, k_ref[0:r0, c0:c0 + hd], contract_last,
                                     preferred_element_type=F32)
            m = jnp.maximum(m, jnp.max(s_main, axis=-1, keepdims=True))
        scores[w] = (s_diag, s_main, m)

    def stage_probs(w):
        s_diag, s_main, m = scores.pop(w)
        p_diag = jnp.exp2((s_diag - m) * exp_scale)
        l = jnp.sum(p_diag, axis=-1, keepdims=True)
        p_main = None
        if s_main is not None:
            p_main = jnp.exp2((s_main - m) * exp_scale)
            l = l + jnp.sum(p_main, axis=-1, keepdims=True)
            p_main = p_main.astype(BF16)
        probs[w] = (p_diag.astype(BF16), p_main, l)

    def stage_values(w):
        qi, half = items[w]
        r0 = qi * tq
        p_diag, p_main, l = probs.pop(w)
        acc = jnp.dot(p_diag, v_ref[r0:r0 + tq, :], preferred_element_type=F32)
        if p_main is not None:
            acc = acc + jnp.dot(p_main, v_ref[0:r0, :], preferred_element_type=F32)
        outs[w] = acc * (1.0 / l)
        if half == 1:
            o = outs.pop(w - 1) - lam * outs.pop(w)
            o_ref[r0:r0 + tq, :] = ((_rms(o) * gain_ref[...]) * (1.0 - lam_init)).astype(BF16)

    n_q = len(items) // 2
    for t in range(n_q + 2):
        for half in range(2):
            if t < n_q:
                stage_scores(2 * t + half)
        for half in range(2):
            if 0 <= t - 1 < n_q:
                stage_probs(2 * (t - 1) + half)
        for half in range(2):
            if 0 <= t - 2 < n_q:
                stage_values(2 * (t - 2) + half)


def _attention(q, kv, lam_params, subln_gain, *, lam_init):
    bsz, seq, _ = q.shape
    w = 2 * DIFF_HEAD_DIM
    kern = functools.partial(_attn_kernel, lam_init=lam_init)
    return pl.pallas_call(
        kern,
        grid=(bsz, DIFF_HEADS),
        in_specs=[
            pl.BlockSpec((4, DIFF_HEAD_DIM), lambda b, h: (0, 0)),
            pl.BlockSpec((None, seq, w), lambda b, h: (b, 0, h)),
            pl.BlockSpec((None, seq, w), lambda b, h: (b, 0, h)),
            pl.BlockSpec((None, seq, DIFF_V_DIM), lambda b, h: (b, 0, DIFF_HEADS + h)),
            pl.BlockSpec((1, DIFF_V_DIM), lambda b, h: (0, 0)),
        ],
        out_specs=pl.BlockSpec((None, seq, DIFF_V_DIM), lambda b, h: (b, 0, h)),
        out_shape=jax.ShapeDtypeStruct((bsz, seq, DIFF_HEADS * DIFF_V_DIM), BF16),
        compiler_params=_cparams(("arbitrary", "arbitrary")),
        name="diffattn",
    )(lam_params, q, kv, kv, subln_gain)


def kernel(x, c, norm_gain, w_ada, b_ada, ret_w_in, ret_w_out, ffn_w_in, ffn_w_conv,
           ffn_b_conv, ffn_w_down, kv_norm_gain, kv_w_ada, kv_b_ada, w_kv, diff_w_q,
           diff_lambda, diff_subln_gain, diff_w_out, final_norm_gain):
    bsz, seq, d = x.shape
    t = bsz * seq
    depth = w_ada.shape[0]
    n_a = ret_w_in.shape[0]
    assert depth == 2 and n_a == 1 and diff_w_q.shape[0] == 1
    assert seq % PROJ_TM == 0 and seq % FFN_TM == 0
    assert seq % ATT_TQ == 0 and PROJ_TM % RET_KCHUNK == 0
    assert RET_KCHUNK & (RET_KCHUNK - 1) == 0

    pos = jnp.arange(seq, dtype=F32)[:, None]
    ret_freqs = 1.0 / (ROPE_THETA ** jnp.linspace(0.0, 1.0, LANES))
    ret_ang = pos * ret_freqs[None, :]
    ret_cos, ret_sin = jnp.cos(ret_ang), jnp.sin(ret_ang)
    diff_freqs = 1.0 / (ROPE_THETA ** (jnp.arange(0, DIFF_HEAD_DIM, 2, dtype=F32) / DIFF_HEAD_DIM))
    diff_ang = pos * diff_freqs[None, :]
    dcos, dsin = jnp.cos(diff_ang), jnp.sin(diff_ang)
    diff_cos = jnp.concatenate([dcos, dcos], axis=1)
    diff_sin = jnp.concatenate([-dsin, dsin], axis=1)

    xf = x.reshape(t, d)
    gains = norm_gain.reshape(depth * 2, 1, d)
    b_ada3 = b_ada.reshape(depth, 1, 6 * d)
    ffn_b3 = ffn_b_conv.reshape(depth, 1, -1)
    mod0 = _ada(c, w_ada, b_ada3, 0).reshape(bsz, 1, 6 * d)
    mod1 = _ada(c, w_ada, b_ada3, 1).reshape(bsz, 1, 6 * d)
    kvmod = _ada(c, kv_w_ada[None], kv_b_ada.reshape(1, 1, 2 * d), 0).reshape(bsz, 1, 2 * d)

    qk_cols = 2 * RET_HEADS * RET_QK_DIM
    (qkvg,) = _proj(xf, [(gains, 0, mod0, 1, 0, ret_w_in)], ret_cos, ret_sin, seq=seq,
                    rope_mode="ret", rope_cols=qk_cols,
                    silu_from=qk_cols + RET_HEADS * RET_V_DIM)
    y = _retention(qkvg.reshape(bsz, seq, -1), bsz, seq).reshape(t, -1)
    xf = _ffn(y, xf, gains, 1, mod0, ret_w_out, ffn_w_in, ffn_w_conv, ffn_b3, ffn_w_down,
              final_norm_gain[None], 0, seq=seq, final_norm=False)

    k_dim = DIFF_HEADS * 2 * DIFF_HEAD_DIM
    kv, q = _proj(xf, [(kv_norm_gain.reshape(1, 1, d), 0, kvmod, 1, 0, w_kv[None]),
                       (gains, 2, mod1, 1, 0, diff_w_q)],
                  diff_cos, diff_sin, seq=seq, rope_mode="diff", rope_cols=k_dim)

    lam_init = 0.8 - 0.6 * math.exp(-0.3 * 1)
    o = _attention(q.reshape(bsz, seq, -1), kv.reshape(bsz, seq, -1), diff_lambda[0],
                   diff_subln_gain[0][None], lam_init=lam_init).reshape(t, -1)
    xf = _ffn(o, xf, gains, 3, mod1, diff_w_out, ffn_w_in, ffn_w_conv, ffn_b3, ffn_w_down,
              final_norm_gain[None], 1, seq=seq, final_norm=True)
    return xf.reshape(bsz, seq, d)
```

```python
import functools
import math

import jax
import jax.numpy as jnp
from jax import lax
from jax.experimental import pallas as pl
from jax.experimental.pallas import tpu as pltpu

F32 = jnp.float32
BF16 = jnp.bfloat16

RET_HEADS = 4
RET_QK_DIM = 256
RET_V_DIM = 512
DIFF_HEADS = 4
DIFF_HEAD_DIM = 128
DIFF_V_DIM = 256
CONV_WIDTH = 3
ROPE_THETA = 10000.0
NORM_EPS = 1e-6

LANES = 128
SUBLANES = 8
BF16_ROWS = 16
VMEM_LIMIT_BYTES = 56 * 1024 * 1024

ADA_TN = 1024
PROJ_RET_TM = 512
PROJ_DIFF_TM = 1024
PROJ_TN = 512
RET_KCHUNK = 256
RET_HEADS_PER_STEP = 2
FFN_TM = 512
FFN_TF = 256
ATT_TQ = 256
WEIGHT_STAGE_BYTES = 768 * 1024
WEIGHT_STAGE_SLOTS = 5


def _cparams(semantics):
    return pltpu.CompilerParams(dimension_semantics=semantics,
                                vmem_limit_bytes=VMEM_LIMIT_BYTES)


def _rms(x):
    return x * lax.rsqrt(jnp.mean(x * x, axis=-1, keepdims=True) + NORM_EPS)


def _prenorm(x, gain, sc, sh):
    return (_rms(x) * gain) * (1.0 + sc) + sh


def _sigmoid(x):
    return 1.0 / (1.0 + jnp.exp(-x))


def _hbm_weight_spec():
    return pl.BlockSpec(memory_space=pl.ANY)


def _staged_weight_scratch(w):
    _, k, n = w.shape
    rows = max(r for r in range(BF16_ROWS, k + 1, BF16_ROWS)
               if k % r == 0 and r * n * 4 <= WEIGHT_STAGE_BYTES)
    return [pltpu.VMEM((k, n), BF16), pltpu.VMEM((WEIGHT_STAGE_SLOTS, rows, n), F32),
            pltpu.SemaphoreType.DMA((WEIGHT_STAGE_SLOTS,))]


def _stage_weight_bf16(w_hbm, layer, dst_ref, stage_ref, sem):
    slots, rows = stage_ref.shape[0], stage_ref.shape[1]
    n_chunks = dst_ref.shape[0] // rows

    def chunk_copy(c):
        return pltpu.make_async_copy(w_hbm.at[layer, pl.ds(c * rows, rows), :],
                                     stage_ref.at[c % slots], sem.at[c % slots])

    for c in range(min(slots - 1, n_chunks)):
        chunk_copy(c).start()
    for c in range(n_chunks):
        ahead = c + slots - 1
        if ahead < n_chunks:
            chunk_copy(ahead).start()
        chunk_copy(c).wait()
        dst_ref[c * rows:(c + 1) * rows, :] = stage_ref[c % slots].astype(BF16)


def _ada_kernel(c_ref, w_ref, b_ref, o_ref):
    c = c_ref[...]
    s = (c * _sigmoid(c)).astype(BF16)
    o_ref[...] = jnp.dot(s, w_ref[...].astype(BF16), preferred_element_type=F32) + b_ref[...]


def _ada(c, w, b, layer):
    bsz, d = c.shape
    n = w.shape[-1]
    return pl.pallas_call(
        _ada_kernel,
        grid=(n // ADA_TN,),
        in_specs=[
            pl.BlockSpec((bsz, d), lambda j: (0, 0)),
            pl.BlockSpec((None, d, ADA_TN), lambda j: (layer, 0, j)),
            pl.BlockSpec((None, 1, ADA_TN), lambda j: (layer, 0, j)),
        ],
        out_specs=pl.BlockSpec((bsz, ADA_TN), lambda j: (0, j)),
        out_shape=jax.ShapeDtypeStruct((bsz, n), F32),
        compiler_params=_cparams(("arbitrary",)),
        name="ada",
    )(c, w, b)


def _rope_ret(acc, lo, cos, sin, pos1, o_ref):
    qk_w = RET_HEADS * RET_QK_DIM
    for hh in range(PROJ_TN // RET_QK_DIM):
        c0 = hh * RET_QK_DIM
        col = lo + c0
        head = (col % qk_w) // RET_QK_DIM
        log_gamma = math.log(1.0 - 2.0 ** (-5.0 - head))
        if col < qk_w:
            dec = jnp.exp(pos1 * log_gamma)
        else:
            dec = jnp.exp(pos1 * (-log_gamma)) * (RET_QK_DIM ** -0.5)
        x1 = acc[:, c0:c0 + LANES]
        x2 = acc[:, c0 + LANES:c0 + 2 * LANES]
        o_ref[:, col:col + LANES] = ((x1 * cos - x2 * sin) * dec).astype(BF16)
        o_ref[:, col + LANES:col + 2 * LANES] = ((x2 * cos + x1 * sin) * dec).astype(BF16)


def _rope_diff(acc, lo, cos, sin, o_ref):
    for bb in range(PROJ_TN // LANES):
        c0 = bb * LANES
        xb = acc[:, c0:c0 + LANES]
        o_ref[:, lo + c0:lo + c0 + LANES] = (
            xb * cos + pltpu.roll(xb, LANES // 2, 1) * sin).astype(BF16)


def _proj_kernel(*refs, n_branch, rope_mode, rope_cols, silu_from):
    x_ref = refs[0]
    cos = refs[1 + 4 * n_branch][...]
    sin = refs[2 + 4 * n_branch][...]
    out_refs = refs[3 + 4 * n_branch:3 + 5 * n_branch]
    scratch = refs[3 + 5 * n_branch:]
    tm = x_ref.shape[0]

    @pl.when(pl.program_id(0) == 0)
    def _():
        for br in range(n_branch):
            _stage_weight_bf16(refs[4 + 4 * br], 0, *scratch[3 * br:3 * br + 3])

    normed = _rms(x_ref[...])
    if rope_mode == "ret":
        row = lax.broadcasted_iota(jnp.int32, (tm, 1), 0)
        pos1 = (jnp.bitwise_and(row, RET_KCHUNK - 1) + 1).astype(F32)
    for br in range(n_branch):
        gain_ref, sc_ref, sh_ref = refs[1 + 4 * br:4 + 4 * br]
        w_ref = scratch[3 * br]
        o_ref = out_refs[br]
        h = ((normed * gain_ref[...]) * (1.0 + sc_ref[...]) + sh_ref[...]).astype(BF16)
        for j in range(w_ref.shape[1] // PROJ_TN):
            lo = j * PROJ_TN
            acc = jnp.dot(h, w_ref[:, lo:lo + PROJ_TN], preferred_element_type=F32)
            if lo >= silu_from:
                o_ref[:, lo:lo + PROJ_TN] = (acc * _sigmoid(acc)).astype(BF16)
            elif lo >= rope_cols:
                o_ref[:, lo:lo + PROJ_TN] = acc.astype(BF16)
            elif rope_mode == "ret":
                _rope_ret(acc, lo, cos, sin, pos1, o_ref)
            else:
                _rope_diff(acc, lo, cos, sin, o_ref)


def _proj(x, branches, cos, sin, *, tm, seq, rope_mode, rope_cols, silu_from=None):
    t, d = x.shape
    tiles_per_seq = seq // tm
    in_specs = [pl.BlockSpec((tm, d), lambda i: (i, 0))]
    args = [x]
    out_specs, out_shapes, scratch = [], [], []
    for gains, gain_idx, mod, sc_idx, sh_idx, w in branches:
        n = w.shape[-1]
        in_specs += [
            pl.BlockSpec((None, 1, d), lambda i, g=gain_idx: (g, 0, 0)),
            pl.BlockSpec((None, 1, d), lambda i, c=sc_idx: (i // tiles_per_seq, 0, c)),
            pl.BlockSpec((None, 1, d), lambda i, c=sh_idx: (i // tiles_per_seq, 0, c)),
            _hbm_weight_spec(),
        ]
        args += [gains, mod, mod, w]
        scratch += _staged_weight_scratch(w)
        out_specs.append(pl.BlockSpec((tm, n), lambda i: (i, 0)))
        out_shapes.append(jax.ShapeDtypeStruct((t, n), BF16))
    in_specs += [pl.BlockSpec((tm, LANES), lambda i: (i % tiles_per_seq, 0))] * 2
    args += [cos, sin]
    if silu_from is None:
        silu_from = max(w.shape[-1] for *_, w in branches)
    kern = functools.partial(_proj_kernel, n_branch=len(branches), rope_mode=rope_mode,
                             rope_cols=rope_cols, silu_from=silu_from)
    return pl.pallas_call(
        kern,
        grid=(t // tm,),
        in_specs=in_specs,
        out_specs=out_specs,
        out_shape=out_shapes,
        scratch_shapes=scratch,
        compiler_params=_cparams(("arbitrary",)),
        name="proj_" + rope_mode,
    )(*args)


def _ret_kernel(q_ref, k_ref, v_ref, g_ref, o_ref):
    seq = q_ref.shape[0]
    chunk = RET_KCHUNK
    dk, dv = RET_QK_DIM, RET_V_DIM
    row = lax.broadcasted_iota(jnp.int32, (chunk, chunk), 0)
    col = lax.broadcasted_iota(jnp.int32, (chunk, chunk), 1)
    causal = col <= row
    n_chunks = seq // chunk

    for j in range(RET_HEADS_PER_STEP):
        hf = (pl.program_id(1) * RET_HEADS_PER_STEP + j).astype(F32)
        log_gamma = jnp.log(1.0 - jnp.exp2(jnp.full((1, 1), -5.0, F32) - hf))
        c_dec = jnp.exp(log_gamma * float(chunk))

        def intra(n):
            r0 = n * chunk
            qc = q_ref[r0:r0 + chunk, j * dk:(j + 1) * dk]
            kc = k_ref[r0:r0 + chunk, j * dk:(j + 1) * dk]
            vc = v_ref[r0:r0 + chunk, j * dv:(j + 1) * dv]
            s = lax.dot_general(qc, kc, (((1,), (1,)), ((), ())), preferred_element_type=F32)
            inner = jnp.where(causal, s, 0.0).astype(BF16)
            out = jnp.dot(inner, vc, preferred_element_type=F32)
            kv = None
            if n + 1 < n_chunks:
                kv = lax.dot_general(kc, vc, (((0,), (0,)), ((), ())),
                                     preferred_element_type=F32)
            return out, kv

        state = None
        nxt = intra(0)
        for n in range(n_chunks):
            r0 = n * chunk
            out, kv = nxt
            if n + 1 < n_chunks:
                nxt = intra(n + 1)
            if state is not None:
                out = out + jnp.dot(q_ref[r0:r0 + chunk, j * dk:(j + 1) * dk],
                                    state.astype(BF16), preferred_element_type=F32)
            if kv is not None:
                state = c_dec * (kv if state is None else state + kv)
            gate = g_ref[r0:r0 + chunk, j * dv:(j + 1) * dv].astype(F32)
            o_ref[r0:r0 + chunk, j * dv:(j + 1) * dv] = (gate * _rms(out)).astype(BF16)


def _retention(qkvg, bsz, seq):
    hps = RET_HEADS_PER_STEP
    dk = hps * RET_QK_DIM
    dv = hps * RET_V_DIM
    groups = RET_HEADS // hps
    qk_blocks = RET_HEADS * RET_QK_DIM // dv
    return pl.pallas_call(
        _ret_kernel,
        grid=(bsz, groups),
        in_specs=[
            pl.BlockSpec((None, seq, dk), lambda b, h: (b, 0, h)),
            pl.BlockSpec((None, seq, dk), lambda b, h: (b, 0, groups + h)),
            pl.BlockSpec((None, seq, dv), lambda b, h: (b, 0, 2 * qk_blocks + h)),
            pl.BlockSpec((None, seq, dv), lambda b, h: (b, 0, 2 * qk_blocks + groups + h)),
        ],
        out_specs=pl.BlockSpec((None, seq, dv), lambda b, h: (b, 0, h)),
        out_shape=jax.ShapeDtypeStruct((bsz, seq, RET_HEADS * RET_V_DIM), BF16),
        compiler_params=_cparams(("arbitrary", "arbitrary")),
        name="retention",
    )(qkvg, qkvg, qkvg, qkvg)


def _ffn_kernel(mix_ref, x_ref, g1_ref, gain_ref, sc_ref, sh_ref, g2_ref, wout_hbm, win_hbm,
                wconv_ref, bconv_ref, wdown_hbm, fgain_ref, o_ref, carry_ref, y_ref,
                wout_ref, wout_stage, wout_sem, win_ref, win_stage, win_sem,
                wdown_ref, wdown_stage, wdown_sem, *, layer, tiles_per_seq, final_norm):
    tm = x_ref.shape[0]
    dff = wdown_ref.shape[0]

    @pl.when(pl.program_id(0) == 0)
    def _():
        _stage_weight_bf16(wout_hbm, 0, wout_ref, wout_stage, wout_sem)
        _stage_weight_bf16(win_hbm, layer, win_ref, win_stage, win_sem)
        _stage_weight_bf16(wdown_hbm, layer, wdown_ref, wdown_stage, wdown_sem)

    x = x_ref[...] + g1_ref[...] * jnp.dot(mix_ref[...], wout_ref[...],
                                           preferred_element_type=F32)
    h = _prenorm(x, gain_ref[...], sc_ref[...], sh_ref[...]).astype(BF16)
    seq_start = (pl.program_id(0) % tiles_per_seq) == 0
    sqrt_half = math.sqrt(0.5)

    for f in range(dff // FFN_TF):
        lo = f * FFN_TF
        a = jnp.dot(h, win_ref[:, lo:lo + FFN_TF], preferred_element_type=F32)
        g = jnp.dot(h, win_ref[:, dff + lo:dff + lo + FFN_TF], preferred_element_type=F32)
        w0 = wconv_ref[0:1, lo:lo + FFN_TF]
        w1 = wconv_ref[1:2, lo:lo + FFN_TF]
        w2 = wconv_ref[2:3, lo:lo + FFN_TF]
        bias = bconv_ref[:, lo:lo + FFN_TF]

        def conv_gelu_gate(a0, a1, a2, gate):
            ac = w0 * a2 + w1 * a1 + w2 * a0 + bias
            return (0.5 * ac * (1.0 + lax.erf(ac * sqrt_half))) * gate

        y_ref[:, lo:lo + FFN_TF] = conv_gelu_gate(
            a, pltpu.roll(a, 1, 0), pltpu.roll(a, 2, 0), g).astype(BF16)
        prev = jnp.where(seq_start, 0.0, carry_ref[:, lo:lo + FFN_TF])
        ext = jnp.concatenate([prev, a[0:BF16_ROWS]], axis=0)
        top = conv_gelu_gate(a[0:BF16_ROWS],
                             pltpu.roll(ext, 1, 0)[SUBLANES:],
                             pltpu.roll(ext, 2, 0)[SUBLANES:],
                             g[0:BF16_ROWS])
        y_ref[0:BF16_ROWS, lo:lo + FFN_TF] = top.astype(BF16)
        carry_ref[:, lo:lo + FFN_TF] = a[tm - SUBLANES:tm]

    out = x + g2_ref[...] * jnp.dot(y_ref[...], wdown_ref[...], preferred_element_type=F32)
    if final_norm:
        out = _rms(out) * fgain_ref[...]
    o_ref[...] = out


def _ffn(mix, x, gains, gain_idx, mod, w_out, w_in, w_conv, b_conv, w_down, fgain, layer, *,
         seq, final_norm):
    t, d = x.shape
    kmix = mix.shape[1]
    dff = w_down.shape[1]
    tiles_per_seq = seq // FFN_TM
    kern = functools.partial(_ffn_kernel, layer=layer, tiles_per_seq=tiles_per_seq,
                             final_norm=final_norm)
    return pl.pallas_call(
        kern,
        grid=(t // FFN_TM,),
        in_specs=[
            pl.BlockSpec((FFN_TM, kmix), lambda i: (i, 0)),
            pl.BlockSpec((FFN_TM, d), lambda i: (i, 0)),
            pl.BlockSpec((None, 1, d), lambda i: (i // tiles_per_seq, 0, 2)),
            pl.BlockSpec((None, 1, d), lambda i: (gain_idx, 0, 0)),
            pl.BlockSpec((None, 1, d), lambda i: (i // tiles_per_seq, 0, 4)),
            pl.BlockSpec((None, 1, d), lambda i: (i // tiles_per_seq, 0, 3)),
            pl.BlockSpec((None, 1, d), lambda i: (i // tiles_per_seq, 0, 5)),
            _hbm_weight_spec(),
            _hbm_weight_spec(),
            pl.BlockSpec((None, CONV_WIDTH, dff), lambda i: (layer, 0, 0)),
            pl.BlockSpec((None, 1, dff), lambda i: (layer, 0, 0)),
            _hbm_weight_spec(),
            pl.BlockSpec((1, d), lambda i: (0, 0)),
        ],
        out_specs=pl.BlockSpec((FFN_TM, d), lambda i: (i, 0)),
        out_shape=jax.ShapeDtypeStruct((t, d), F32),
        scratch_shapes=[
            pltpu.VMEM((SUBLANES, dff), F32),
            pltpu.VMEM((FFN_TM, dff), BF16),
        ] + (_staged_weight_scratch(w_out) + _staged_weight_scratch(w_in)
             + _staged_weight_scratch(w_down)),
        compiler_params=_cparams(("arbitrary",)),
        name="convffn",
    )(mix, x, mod, gains, mod, mod, mod, w_out, w_in, w_conv, b_conv, w_down, fgain)


def _attn_kernel(lam_ref, q_ref, k_ref, v_ref, gain_ref, o_ref, *, lam_init):
    seq = q_ref.shape[0]
    tq = ATT_TQ
    hd = DIFF_HEAD_DIM
    exp_scale = (hd ** -0.5) * math.log2(math.e)
    contract_last = (((1,), (1,)), ((), ()))

    lv = lam_ref[...]
    lam = (jnp.exp(jnp.sum(lv[0:1] * lv[1:2], axis=1, keepdims=True))
           - jnp.exp(jnp.sum(lv[2:3] * lv[3:4], axis=1, keepdims=True)) + lam_init)

    row = lax.broadcasted_iota(jnp.int32, (tq, tq), 0)
    col = lax.broadcasted_iota(jnp.int32, (tq, tq), 1)
    causal = col <= row

    items = [(qi, half) for qi in reversed(range(seq // tq)) for half in range(2)]
    scores, probs, outs = {}, {}, {}

    def stage_scores(w):
        qi, half = items[w]
        r0, c0 = qi * tq, half * hd
        qh = q_ref[r0:r0 + tq, c0:c0 + hd]
        s_diag = lax.dot_general(qh, k_ref[r0:r0 + tq, c0:c0 + hd], contract_last,
                                 preferred_element_type=F32)
        s_diag = jnp.where(causal, s_diag, -jnp.inf)
        m = jnp.max(s_diag, axis=-1, keepdims=True)
        s_main = None
        if qi > 0:
            s_main = lax.dot_general(qh, k_ref[0:r0, c0:c0 + hd], contract_last,
                                     preferred_element_type=F32)
            m = jnp.maximum(m, jnp.max(s_main, axis=-1, keepdims=True))
        scores[w] = (s_diag, s_main, m)

    def stage_probs(w):
        s_diag, s_main, m = scores.pop(w)
        p_diag = jnp.exp2((s_diag - m) * exp_scale)
        l = jnp.sum(p_diag, axis=-1, keepdims=True)
        p_main = None
        if s_main is not None:
            p_main = jnp.exp2((s_main - m) * exp_scale)
            l = l + jnp.sum(p_main, axis=-1, keepdims=True)
            p_main = p_main.astype(BF16)
        probs[w] = (p_diag.astype(BF16), p_main, l)

    def stage_values(w):
        qi, half = items[w]
        r0 = qi * tq
        p_diag, p_main, l = probs.pop(w)
        acc = jnp.dot(p_diag, v_ref[r0:r0 + tq, :], preferred_element_type=F32)
        if p_main is not None:
            acc = acc + jnp.dot(p_main, v_ref[0:r0, :], preferred_element_type=F32)
        outs[w] = acc * (1.0 / l)
        if half == 1:
            o = outs.pop(w - 1) - lam * outs.pop(w)
            o_ref[r0:r0 + tq, :] = ((_rms(o) * gain_ref[...]) * (1.0 - lam_init)).astype(BF16)

    n_q = len(items) // 2
    for t in range(n_q + 2):
        for half in range(2):
            if t < n_q:
                stage_scores(2 * t + half)
        for half in range(2):
            if 0 <= t - 1 < n_q:
                stage_probs(2 * (t - 1) + half)
        for half in range(2):
            if 0 <= t - 2 < n_q:
                stage_values(2 * (t - 2) + half)


def _attention(q, kv, lam_params, subln_gain, *, lam_init):
    bsz, seq, _ = q.shape
    w = 2 * DIFF_HEAD_DIM
    kern = functools.partial(_attn_kernel, lam_init=lam_init)
    return pl.pallas_call(
        kern,
        grid=(bsz, DIFF_HEADS),
        in_specs=[
            pl.BlockSpec((4, DIFF_HEAD_DIM), lambda b, h: (0, 0)),
            pl.BlockSpec((None, seq, w), lambda b, h: (b, 0, h)),
            pl.BlockSpec((None, seq, w), lambda b, h: (b, 0, h)),
            pl.BlockSpec((None, seq, DIFF_V_DIM), lambda b, h: (b, 0, DIFF_HEADS + h)),
            pl.BlockSpec((1, DIFF_V_DIM), lambda b, h: (0, 0)),
        ],
        out_specs=pl.BlockSpec((None, seq, DIFF_V_DIM), lambda b, h: (b, 0, h)),
        out_shape=jax.ShapeDtypeStruct((bsz, seq, DIFF_HEADS * DIFF_V_DIM), BF16),
        compiler_params=_cparams(("arbitrary", "arbitrary")),
        name="diffattn",
    )(lam_params, q, kv, kv, subln_gain)


def kernel(x, c, norm_gain, w_ada, b_ada, ret_w_in, ret_w_out, ffn_w_in, ffn_w_conv,
           ffn_b_conv, ffn_w_down, kv_norm_gain, kv_w_ada, kv_b_ada, w_kv, diff_w_q,
           diff_lambda, diff_subln_gain, diff_w_out, final_norm_gain):
    bsz, seq, d = x.shape
    t = bsz * seq
    depth = w_ada.shape[0]
    n_a = ret_w_in.shape[0]
    assert depth == 2 and n_a == 1 and diff_w_q.shape[0] == 1
    assert seq % PROJ_RET_TM == 0 and seq % PROJ_DIFF_TM == 0 and seq % FFN_TM == 0
    assert seq % ATT_TQ == 0 and PROJ_RET_TM % RET_KCHUNK == 0
    assert RET_KCHUNK & (RET_KCHUNK - 1) == 0

    pos = jnp.arange(seq, dtype=F32)[:, None]
    ret_freqs = 1.0 / (ROPE_THETA ** jnp.linspace(0.0, 1.0, LANES))
    ret_ang = pos * ret_freqs[None, :]
    ret_cos, ret_sin = jnp.cos(ret_ang), jnp.sin(ret_ang)
    diff_freqs = 1.0 / (ROPE_THETA ** (jnp.arange(0, DIFF_HEAD_DIM, 2, dtype=F32) / DIFF_HEAD_DIM))
    diff_ang = pos * diff_freqs[None, :]
    dcos, dsin = jnp.cos(diff_ang), jnp.sin(diff_ang)
    diff_cos = jnp.concatenate([dcos, dcos], axis=1)
    diff_sin = jnp.concatenate([-dsin, dsin], axis=1)

    xf = x.reshape(t, d)
    gains = norm_gain.reshape(depth * 2, 1, d)
    b_ada3 = b_ada.reshape(depth, 1, 6 * d)
    ffn_b3 = ffn_b_conv.reshape(depth, 1, -1)
    mod0 = _ada(c, w_ada, b_ada3, 0).reshape(bsz, 1, 6 * d)
    mod1 = _ada(c, w_ada, b_ada3, 1).reshape(bsz, 1, 6 * d)
    kvmod = _ada(c, kv_w_ada[None], kv_b_ada.reshape(1, 1, 2 * d), 0).reshape(bsz, 1, 2 * d)

    qk_cols = 2 * RET_HEADS * RET_QK_DIM
    (qkvg,) = _proj(xf, [(gains, 0, mod0, 1, 0, ret_w_in)], ret_cos, ret_sin,
                    tm=PROJ_RET_TM, seq=seq, rope_mode="ret", rope_cols=qk_cols,
                    silu_from=qk_cols + RET_HEADS * RET_V_DIM)
    y = _retention(qkvg.reshape(bsz, seq, -1), bsz, seq).reshape(t, -1)
    xf = _ffn(y, xf, gains, 1, mod0, ret_w_out, ffn_w_in, ffn_w_conv, ffn_b3, ffn_w_down,
              final_norm_gain[None], 0, seq=seq, final_norm=False)

    k_dim = DIFF_HEADS * 2 * DIFF_HEAD_DIM
    kv, q = _proj(xf, [(kv_norm_gain.reshape(1, 1, d), 0, kvmod, 1, 0, w_kv[None]),
                       (gains, 2, mod1, 1, 0, diff_w_q)],
                  diff_cos, diff_sin, tm=PROJ_DIFF_TM, seq=seq, rope_mode="diff",
                  rope_cols=k_dim)

    lam_init = 0.8 - 0.6 * math.exp(-0.3 * 1)
    o = _attention(q.reshape(bsz, seq, -1), kv.reshape(bsz, seq, -1), diff_lambda[0],
                   diff_subln_gain[0][None], lam_init=lam_init).reshape(t, -1)
    xf = _ffn(o, xf, gains, 3, mod1, diff_w_out, ffn_w_in, ffn_w_conv, ffn_b3, ffn_w_down,
              final_norm_gain[None], 1, seq=seq, final_norm=True)
    return xf.reshape(bsz, seq, d)
```

```python
import functools
import math

import jax
import jax.numpy as jnp
from jax import lax
from jax.experimental import pallas as pl
from jax.experimental.pallas import tpu as pltpu

F32 = jnp.float32
BF16 = jnp.bfloat16

RET_HEADS = 4
RET_QK_DIM = 256
RET_V_DIM = 512
DIFF_HEADS = 4
DIFF_HEAD_DIM = 128
DIFF_V_DIM = 256
CONV_WIDTH = 3
ROPE_THETA = 10000.0
NORM_EPS = 1e-6

LANES = 128
SUBLANES = 8
BF16_ROWS = 16
VMEM_LIMIT_BYTES = 56 * 1024 * 1024

ADA_TN = 1024
PROJ_RET_TM = 512
PROJ_DIFF_TM = 1024
PROJ_TN = 512
RET_KCHUNK = 256
RET_HEADS_PER_STEP = 2
FFN_TM = 512
FFN_TF = 256
ATT_TQ = 256
WEIGHT_STAGE_BYTES = 3 * 1024 * 1024
WEIGHT_STAGE_SLOTS = 2


def _cparams(semantics):
    return pltpu.CompilerParams(dimension_semantics=semantics,
                                vmem_limit_bytes=VMEM_LIMIT_BYTES)


def _rms(x):
    return x * lax.rsqrt(jnp.mean(x * x, axis=-1, keepdims=True) + NORM_EPS)


def _prenorm(x, gain, sc, sh):
    return (_rms(x) * gain) * (1.0 + sc) + sh


def _sigmoid(x):
    return 1.0 / (1.0 + jnp.exp(-x))


def _hbm_weight_spec():
    return pl.BlockSpec(memory_space=pl.ANY)


def _stage_rows(k, n):
    return max(r for r in range(BF16_ROWS, k + 1, BF16_ROWS)
               if k % r == 0 and r * n * 4 <= WEIGHT_STAGE_BYTES)


def _staging_scratch(weights):
    n = weights[0].shape[2]
    assert all(w.shape[2] == n for w in weights)
    rows = max(_stage_rows(w.shape[1], n) for w in weights)
    return [pltpu.VMEM((WEIGHT_STAGE_SLOTS, rows, n), F32),
            pltpu.SemaphoreType.DMA((WEIGHT_STAGE_SLOTS,))]


def _bf16_weight_scratch(w):
    return pltpu.VMEM(w.shape[1:], BF16)


def _stage_weight_bf16(w_hbm, layer, dst_ref, stage_ref, sem):
    k, n = dst_ref.shape
    slots = stage_ref.shape[0]
    rows = _stage_rows(k, n)
    n_chunks = k // rows

    def chunk_copy(c):
        return pltpu.make_async_copy(w_hbm.at[layer, pl.ds(c * rows, rows), :],
                                     stage_ref.at[c % slots, pl.ds(0, rows), :],
                                     sem.at[c % slots])

    for c in range(min(slots - 1, n_chunks)):
        chunk_copy(c).start()
    for c in range(n_chunks):
        ahead = c + slots - 1
        if ahead < n_chunks:
            chunk_copy(ahead).start()
        chunk_copy(c).wait()
        dst_ref[c * rows:(c + 1) * rows, :] = stage_ref[c % slots, 0:rows, :].astype(BF16)


def _ada_kernel(c_ref, w_ref, b_ref, o_ref):
    c = c_ref[...]
    s = (c * _sigmoid(c)).astype(BF16)
    o_ref[...] = jnp.dot(s, w_ref[...].astype(BF16), preferred_element_type=F32) + b_ref[...]


def _ada(c, w, b, layer):
    bsz, d = c.shape
    n = w.shape[-1]
    return pl.pallas_call(
        _ada_kernel,
        grid=(n // ADA_TN,),
        in_specs=[
            pl.BlockSpec((bsz, d), lambda j: (0, 0)),
            pl.BlockSpec((None, d, ADA_TN), lambda j: (layer, 0, j)),
            pl.BlockSpec((None, 1, ADA_TN), lambda j: (layer, 0, j)),
        ],
        out_specs=pl.BlockSpec((bsz, ADA_TN), lambda j: (0, j)),
        out_shape=jax.ShapeDtypeStruct((bsz, n), F32),
        compiler_params=_cparams(("arbitrary",)),
        name="ada",
    )(c, w, b)


def _rope_ret(acc, lo, cos, sin, pos1, o_ref):
    qk_w = RET_HEADS * RET_QK_DIM
    for hh in range(PROJ_TN // RET_QK_DIM):
        c0 = hh * RET_QK_DIM
        col = lo + c0
        head = (col % qk_w) // RET_QK_DIM
        log_gamma = math.log(1.0 - 2.0 ** (-5.0 - head))
        if col < qk_w:
            dec = jnp.exp(pos1 * log_gamma)
        else:
            dec = jnp.exp(pos1 * (-log_gamma)) * (RET_QK_DIM ** -0.5)
        x1 = acc[:, c0:c0 + LANES]
        x2 = acc[:, c0 + LANES:c0 + 2 * LANES]
        o_ref[:, col:col + LANES] = ((x1 * cos - x2 * sin) * dec).astype(BF16)
        o_ref[:, col + LANES:col + 2 * LANES] = ((x2 * cos + x1 * sin) * dec).astype(BF16)


def _rope_diff(acc, lo, cos, sin, o_ref):
    for bb in range(PROJ_TN // LANES):
        c0 = bb * LANES
        xb = acc[:, c0:c0 + LANES]
        o_ref[:, lo + c0:lo + c0 + LANES] = (
            xb * cos + pltpu.roll(xb, LANES // 2, 1) * sin).astype(BF16)


def _proj_kernel(*refs, n_branch, rope_mode, rope_cols, silu_from):
    x_ref = refs[0]
    cos = refs[1 + 4 * n_branch][...]
    sin = refs[2 + 4 * n_branch][...]
    out_refs = refs[3 + 4 * n_branch:3 + 5 * n_branch]
    scratch = refs[3 + 5 * n_branch:]
    tm = x_ref.shape[0]

    @pl.when(pl.program_id(0) == 0)
    def _():
        for br in range(n_branch):
            _stage_weight_bf16(refs[4 + 4 * br], 0, *scratch[3 * br:3 * br + 3])

    normed = _rms(x_ref[...])
    if rope_mode == "ret":
        row = lax.broadcasted_iota(jnp.int32, (tm, 1), 0)
        pos1 = (jnp.bitwise_and(row, RET_KCHUNK - 1) + 1).astype(F32)
    for br in range(n_branch):
        gain_ref, sc_ref, sh_ref = refs[1 + 4 * br:4 + 4 * br]
        w_ref = scratch[3 * br]
        o_ref = out_refs[br]
        h = ((normed * gain_ref[...]) * (1.0 + sc_ref[...]) + sh_ref[...]).astype(BF16)
        for j in range(w_ref.shape[1] // PROJ_TN):
            lo = j * PROJ_TN
            acc = jnp.dot(h, w_ref[:, lo:lo + PROJ_TN], preferred_element_type=F32)
            if lo >= silu_from:
                o_ref[:, lo:lo + PROJ_TN] = (acc * _sigmoid(acc)).astype(BF16)
            elif lo >= rope_cols:
                o_ref[:, lo:lo + PROJ_TN] = acc.astype(BF16)
            elif rope_mode == "ret":
                _rope_ret(acc, lo, cos, sin, pos1, o_ref)
            else:
                _rope_diff(acc, lo, cos, sin, o_ref)


def _proj(x, branches, cos, sin, *, tm, seq, rope_mode, rope_cols, silu_from=None):
    t, d = x.shape
    tiles_per_seq = seq // tm
    in_specs = [pl.BlockSpec((tm, d), lambda i: (i, 0))]
    args = [x]
    out_specs, out_shapes, scratch = [], [], []
    for gains, gain_idx, mod, sc_idx, sh_idx, w in branches:
        n = w.shape[-1]
        in_specs += [
            pl.BlockSpec((None, 1, d), lambda i, g=gain_idx: (g, 0, 0)),
            pl.BlockSpec((None, 1, d), lambda i, c=sc_idx: (i // tiles_per_seq, 0, c)),
            pl.BlockSpec((None, 1, d), lambda i, c=sh_idx: (i // tiles_per_seq, 0, c)),
            _hbm_weight_spec(),
        ]
        args += [gains, mod, mod, w]
        scratch += [_bf16_weight_scratch(w)] + _staging_scratch([w])
        out_specs.append(pl.BlockSpec((tm, n), lambda i: (i, 0)))
        out_shapes.append(jax.ShapeDtypeStruct((t, n), BF16))
    in_specs += [pl.BlockSpec((tm, LANES), lambda i: (i % tiles_per_seq, 0))] * 2
    args += [cos, sin]
    if silu_from is None:
        silu_from = max(w.shape[-1] for *_, w in branches)
    kern = functools.partial(_proj_kernel, n_branch=len(branches), rope_mode=rope_mode,
                             rope_cols=rope_cols, silu_from=silu_from)
    return pl.pallas_call(
        kern,
        grid=(t // tm,),
        in_specs=in_specs,
        out_specs=out_specs,
        out_shape=out_shapes,
        scratch_shapes=scratch,
        compiler_params=_cparams(("arbitrary",)),
        name="proj_" + rope_mode,
    )(*args)


def _ret_kernel(q_ref, k_ref, v_ref, g_ref, o_ref):
    seq = q_ref.shape[0]
    chunk = RET_KCHUNK
    dk, dv = RET_QK_DIM, RET_V_DIM
    row = lax.broadcasted_iota(jnp.int32, (chunk, chunk), 0)
    col = lax.broadcasted_iota(jnp.int32, (chunk, chunk), 1)
    causal = col <= row
    n_chunks = seq // chunk

    for j in range(RET_HEADS_PER_STEP):
        hf = (pl.program_id(1) * RET_HEADS_PER_STEP + j).astype(F32)
        log_gamma = jnp.log(1.0 - jnp.exp2(jnp.full((1, 1), -5.0, F32) - hf))
        c_dec = jnp.exp(log_gamma * float(chunk))

        def intra(n):
            r0 = n * chunk
            qc = q_ref[r0:r0 + chunk, j * dk:(j + 1) * dk]
            kc = k_ref[r0:r0 + chunk, j * dk:(j + 1) * dk]
            vc = v_ref[r0:r0 + chunk, j * dv:(j + 1) * dv]
            s = lax.dot_general(qc, kc, (((1,), (1,)), ((), ())), preferred_element_type=F32)
            inner = jnp.where(causal, s, 0.0).astype(BF16)
            out = jnp.dot(inner, vc, preferred_element_type=F32)
            kv = None
            if n + 1 < n_chunks:
                kv = lax.dot_general(kc, vc, (((0,), (0,)), ((), ())),
                                     preferred_element_type=F32)
            return out, kv

        state = None
        nxt = intra(0)
        for n in range(n_chunks):
            r0 = n * chunk
            out, kv = nxt
            if n + 1 < n_chunks:
                nxt = intra(n + 1)
            if state is not None:
                out = out + jnp.dot(q_ref[r0:r0 + chunk, j * dk:(j + 1) * dk],
                                    state.astype(BF16), preferred_element_type=F32)
            if kv is not None:
                state = c_dec * (kv if state is None else state + kv)
            gate = g_ref[r0:r0 + chunk, j * dv:(j + 1) * dv].astype(F32)
            o_ref[r0:r0 + chunk, j * dv:(j + 1) * dv] = (gate * _rms(out)).astype(BF16)


def _retention(qkvg, bsz, seq):
    hps = RET_HEADS_PER_STEP
    dk = hps * RET_QK_DIM
    dv = hps * RET_V_DIM
    groups = RET_HEADS // hps
    qk_blocks = RET_HEADS * RET_QK_DIM // dv
    return pl.pallas_call(
        _ret_kernel,
        grid=(bsz, groups),
        in_specs=[
            pl.BlockSpec((None, seq, dk), lambda b, h: (b, 0, h)),
            pl.BlockSpec((None, seq, dk), lambda b, h: (b, 0, groups + h)),
            pl.BlockSpec((None, seq, dv), lambda b, h: (b, 0, 2 * qk_blocks + h)),
            pl.BlockSpec((None, seq, dv), lambda b, h: (b, 0, 2 * qk_blocks + groups + h)),
        ],
        out_specs=pl.BlockSpec((None, seq, dv), lambda b, h: (b, 0, h)),
        out_shape=jax.ShapeDtypeStruct((bsz, seq, RET_HEADS * RET_V_DIM), BF16),
        compiler_params=_cparams(("arbitrary", "arbitrary")),
        name="retention",
    )(qkvg, qkvg, qkvg, qkvg)


def _ffn_kernel(mix_ref, x_ref, g1_ref, gain_ref, sc_ref, sh_ref, g2_ref, wout_hbm, win_hbm,
                wconv_ref, bconv_ref, wdown_hbm, fgain_ref, o_ref, carry_ref, y_ref,
                wout_ref, win_ref, wdown_ref, wd_stage, wd_sem, win_stage, win_sem, *,
                layer, tiles_per_seq, final_norm):
    tm = x_ref.shape[0]
    dff = wdown_ref.shape[0]

    @pl.when(pl.program_id(0) == 0)
    def _():
        _stage_weight_bf16(wout_hbm, 0, wout_ref, wd_stage, wd_sem)
        _stage_weight_bf16(win_hbm, layer, win_ref, win_stage, win_sem)
        _stage_weight_bf16(wdown_hbm, layer, wdown_ref, wd_stage, wd_sem)

    x = x_ref[...] + g1_ref[...] * jnp.dot(mix_ref[...], wout_ref[...],
                                           preferred_element_type=F32)
    h = _prenorm(x, gain_ref[...], sc_ref[...], sh_ref[...]).astype(BF16)
    seq_start = (pl.program_id(0) % tiles_per_seq) == 0
    sqrt_half = math.sqrt(0.5)

    for f in range(dff // FFN_TF):
        lo = f * FFN_TF
        a = jnp.dot(h, win_ref[:, lo:lo + FFN_TF], preferred_element_type=F32)
        g = jnp.dot(h, win_ref[:, dff + lo:dff + lo + FFN_TF], preferred_element_type=F32)
        w0 = wconv_ref[0:1, lo:lo + FFN_TF]
        w1 = wconv_ref[1:2, lo:lo + FFN_TF]
        w2 = wconv_ref[2:3, lo:lo + FFN_TF]
        bias = bconv_ref[:, lo:lo + FFN_TF]

        def conv_gelu_gate(a0, a1, a2, gate):
            ac = w0 * a2 + w1 * a1 + w2 * a0 + bias
            return (0.5 * ac * (1.0 + lax.erf(ac * sqrt_half))) * gate

        y_ref[:, lo:lo + FFN_TF] = conv_gelu_gate(
            a, pltpu.roll(a, 1, 0), pltpu.roll(a, 2, 0), g).astype(BF16)
        prev = jnp.where(seq_start, 0.0, carry_ref[:, lo:lo + FFN_TF])
        ext = jnp.concatenate([prev, a[0:BF16_ROWS]], axis=0)
        top = conv_gelu_gate(a[0:BF16_ROWS],
                             pltpu.roll(ext, 1, 0)[SUBLANES:],
                             pltpu.roll(ext, 2, 0)[SUBLANES:],
                             g[0:BF16_ROWS])
        y_ref[0:BF16_ROWS, lo:lo + FFN_TF] = top.astype(BF16)
        carry_ref[:, lo:lo + FFN_TF] = a[tm - SUBLANES:tm]

    out = x + g2_ref[...] * jnp.dot(y_ref[...], wdown_ref[...], preferred_element_type=F32)
    if final_norm:
        out = _rms(out) * fgain_ref[...]
    o_ref[...] = out


def _ffn(mix, x, gains, gain_idx, mod, w_out, w_in, w_conv, b_conv, w_down, fgain, layer, *,
         seq, final_norm):
    t, d = x.shape
    kmix = mix.shape[1]
    dff = w_down.shape[1]
    tiles_per_seq = seq // FFN_TM
    kern = functools.partial(_ffn_kernel, layer=layer, tiles_per_seq=tiles_per_seq,
                             final_norm=final_norm)
    return pl.pallas_call(
        kern,
        grid=(t // FFN_TM,),
        in_specs=[
            pl.BlockSpec((FFN_TM, kmix), lambda i: (i, 0)),
            pl.BlockSpec((FFN_TM, d), lambda i: (i, 0)),
            pl.BlockSpec((None, 1, d), lambda i: (i // tiles_per_seq, 0, 2)),
            pl.BlockSpec((None, 1, d), lambda i: (gain_idx, 0, 0)),
            pl.BlockSpec((None, 1, d), lambda i: (i // tiles_per_seq, 0, 4)),
            pl.BlockSpec((None, 1, d), lambda i: (i // tiles_per_seq, 0, 3)),
            pl.BlockSpec((None, 1, d), lambda i: (i // tiles_per_seq, 0, 5)),
            _hbm_weight_spec(),
            _hbm_weight_spec(),
            pl.BlockSpec((None, CONV_WIDTH, dff), lambda i: (layer, 0, 0)),
            pl.BlockSpec((None, 1, dff), lambda i: (layer, 0, 0)),
            _hbm_weight_spec(),
            pl.BlockSpec((1, d), lambda i: (0, 0)),
        ],
        out_specs=pl.BlockSpec((FFN_TM, d), lambda i: (i, 0)),
        out_shape=jax.ShapeDtypeStruct((t, d), F32),
        scratch_shapes=[
            pltpu.VMEM((SUBLANES, dff), F32),
            pltpu.VMEM((FFN_TM, dff), BF16),
            _bf16_weight_scratch(w_out), _bf16_weight_scratch(w_in), _bf16_weight_scratch(w_down),
        ] + _staging_scratch([w_out, w_down]) + _staging_scratch([w_in]),
        compiler_params=_cparams(("arbitrary",)),
        name="convffn",
    )(mix, x, mod, gains, mod, mod, mod, w_out, w_in, w_conv, b_conv, w_down, fgain)


def _attn_kernel(lam_ref, q_ref, k_ref, v_ref, gain_ref, o_ref, *, lam_init):
    seq = q_ref.shape[0]
    tq = ATT_TQ
    hd = DIFF_HEAD_DIM
    exp_scale = (hd ** -0.5) * math.log2(math.e)
    contract_last = (((1,), (1,)), ((), ()))

    lv = lam_ref[...]
    lam = (jnp.exp(jnp.sum(lv[0:1] * lv[1:2], axis=1, keepdims=True))
           - jnp.exp(jnp.sum(lv[2:3] * lv[3:4], axis=1, keepdims=True)) + lam_init)

    row = lax.broadcasted_iota(jnp.int32, (tq, tq), 0)
    col = lax.broadcasted_iota(jnp.int32, (tq, tq), 1)
    causal = col <= row

    items = [(qi, half) for qi in reversed(range(seq // tq)) for half in range(2)]
    scores, probs, outs = {}, {}, {}

    def stage_scores(w):
        qi, half = items[w]
        r0, c0 = qi * tq, half * hd
        qh = q_ref[r0:r0 + tq, c0:c0 + hd]
        s_diag = lax.dot_general(qh, k_ref[r0:r0 + tq, c0:c0 + hd], contract_last,
                                 preferred_element_type=F32)
        s_diag = jnp.where(causal, s_diag, -jnp.inf)
        m = jnp.max(s_diag, axis=-1, keepdims=True)
        s_main = None
        if qi > 0:
            s_main = lax.dot_general(qh, k_ref[0:r0, c0:c0 + hd], contract_last,
                                     preferred_element_type=F32)
            m = jnp.maximum(m, jnp.max(s_main, axis=-1, keepdims=True))
        scores[w] = (s_diag, s_main, m)

    def stage_probs(w):
        s_diag, s_main, m = scores.pop(w)
        p_diag = jnp.exp2((s_diag - m) * exp_scale)
        l = jnp.sum(p_diag, axis=-1, keepdims=True)
        p_main = None
        if s_main is not None:
            p_main = jnp.exp2((s_main - m) * exp_scale)
            l = l + jnp.sum(p_main, axis=-1, keepdims=True)
            p_main = p_main.astype(BF16)
        probs[w] = (p_diag.astype(BF16), p_main, l)

    def stage_values(w):
        qi, half = items[w]
        r0 = qi * tq
        p_diag, p_main, l = probs.pop(w)
        acc = jnp.dot(p_diag, v_ref[r0:r0 + tq, :], preferred_element_type=F32)
        if p_main is not None:
            acc = acc + jnp.dot(p_main, v_ref[0:r0, :], preferred_element_type=F32)
        outs[w] = acc * (1.0 / l)
        if half == 1:
            o = outs.pop(w - 1) - lam * outs.pop(w)
            o_ref[r0:r0 + tq, :] = ((_rms(o) * gain_ref[...]) * (1.0 - lam_init)).astype(BF16)

    n_q = len(items) // 2
    for t in range(n_q + 2):
        for half in range(2):
            if t < n_q:
                stage_scores(2 * t + half)
        for half in range(2):
            if 0 <= t - 1 < n_q:
                stage_probs(2 * (t - 1) + half)
        for half in range(2):
            if 0 <= t - 2 < n_q:
                stage_values(2 * (t - 2) + half)


def _attention(q, kv, lam_params, subln_gain, *, lam_init):
    bsz, seq, _ = q.shape
    w = 2 * DIFF_HEAD_DIM
    kern = functools.partial(_attn_kernel, lam_init=lam_init)
    return pl.pallas_call(
        kern,
        grid=(bsz, DIFF_HEADS),
        in_specs=[
            pl.BlockSpec((4, DIFF_HEAD_DIM), lambda b, h: (0, 0)),
            pl.BlockSpec((None, seq, w), lambda b, h: (b, 0, h)),
            pl.BlockSpec((None, seq, w), lambda b, h: (b, 0, h)),
            pl.BlockSpec((None, seq, DIFF_V_DIM), lambda b, h: (b, 0, DIFF_HEADS + h)),
            pl.BlockSpec((1, DIFF_V_DIM), lambda b, h: (0, 0)),
        ],
        out_specs=pl.BlockSpec((None, seq, DIFF_V_DIM), lambda b, h: (b, 0, h)),
        out_shape=jax.ShapeDtypeStruct((bsz, seq, DIFF_HEADS * DIFF_V_DIM), BF16),
        compiler_params=_cparams(("arbitrary", "arbitrary")),
        name="diffattn",
    )(lam_params, q, kv, kv, subln_gain)


def kernel(x, c, norm_gain, w_ada, b_ada, ret_w_in, ret_w_out, ffn_w_in, ffn_w_conv,
           ffn_b_conv, ffn_w_down, kv_norm_gain, kv_w_ada, kv_b_ada, w_kv, diff_w_q,
           diff_lambda, diff_subln_gain, diff_w_out, final_norm_gain):
    bsz, seq, d = x.shape
    t = bsz * seq
    depth = w_ada.shape[0]
    n_a = ret_w_in.shape[0]
    assert depth == 2 and n_a == 1 and diff_w_q.shape[0] == 1
    assert seq % PROJ_RET_TM == 0 and seq % PROJ_DIFF_TM == 0 and seq % FFN_TM == 0
    assert seq % ATT_TQ == 0 and PROJ_RET_TM % RET_KCHUNK == 0
    assert RET_KCHUNK & (RET_KCHUNK - 1) == 0

    pos = jnp.arange(seq, dtype=F32)[:, None]
    ret_freqs = 1.0 / (ROPE_THETA ** jnp.linspace(0.0, 1.0, LANES))
    ret_ang = pos * ret_freqs[None, :]
    ret_cos, ret_sin = jnp.cos(ret_ang), jnp.sin(ret_ang)
    diff_freqs = 1.0 / (ROPE_THETA ** (jnp.arange(0, DIFF_HEAD_DIM, 2, dtype=F32) / DIFF_HEAD_DIM))
    diff_ang = pos * diff_freqs[None, :]
    dcos, dsin = jnp.cos(diff_ang), jnp.sin(diff_ang)
    diff_cos = jnp.concatenate([dcos, dcos], axis=1)
    diff_sin = jnp.concatenate([-dsin, dsin], axis=1)

    xf = x.reshape(t, d)
    gains = norm_gain.reshape(depth * 2, 1, d)
    b_ada3 = b_ada.reshape(depth, 1, 6 * d)
    ffn_b3 = ffn_b_conv.reshape(depth, 1, -1)
    mod0 = _ada(c, w_ada, b_ada3, 0).reshape(bsz, 1, 6 * d)
    mod1 = _ada(c, w_ada, b_ada3, 1).reshape(bsz, 1, 6 * d)
    kvmod = _ada(c, kv_w_ada[None], kv_b_ada.reshape(1, 1, 2 * d), 0).reshape(bsz, 1, 2 * d)

    qk_cols = 2 * RET_HEADS * RET_QK_DIM
    (qkvg,) = _proj(xf, [(gains, 0, mod0, 1, 0, ret_w_in)], ret_cos, ret_sin,
                    tm=PROJ_RET_TM, seq=seq, rope_mode="ret", rope_cols=qk_cols,
                    silu_from=qk_cols + RET_HEADS * RET_V_DIM)
    y = _retention(qkvg.reshape(bsz, seq, -1), bsz, seq).reshape(t, -1)
    xf = _ffn(y, xf, gains, 1, mod0, ret_w_out, ffn_w_in, ffn_w_conv, ffn_b3, ffn_w_down,
              final_norm_gain[None], 0, seq=seq, final_norm=False)

    k_dim = DIFF_HEADS * 2 * DIFF_HEAD_DIM
    kv, q = _proj(xf, [(kv_norm_gain.reshape(1, 1, d), 0, kvmod, 1, 0, w_kv[None]),
                       (gains, 2, mod1, 1, 0, diff_w_q)],
                  diff_cos, diff_sin, tm=PROJ_DIFF_TM, seq=seq, rope_mode="diff",
                  rope_cols=k_dim)

    lam_init = 0.8 - 0.6 * math.exp(-0.3 * 1)
    o = _attention(q.reshape(bsz, seq, -1), kv.reshape(bsz, seq, -1), diff_lambda[0],
                   diff_subln_gain[0][None], lam_init=lam_init).reshape(t, -1)
    xf = _ffn(o, xf, gains, 3, mod1, diff_w_out, ffn_w_in, ffn_w_conv, ffn_b3, ffn_w_down,
              final_norm_gain[None], 1, seq=seq, final_norm=True)
    return xf.reshape(bsz, seq, d)
```

```python
import functools
import math

import jax
import jax.numpy as jnp
from jax import lax
from jax.experimental import pallas as pl
from jax.experimental.pallas import tpu as pltpu

F32 = jnp.float32
BF16 = jnp.bfloat16

RET_HEADS = 4
RET_QK_DIM = 256
RET_V_DIM = 512
DIFF_HEADS = 4
DIFF_HEAD_DIM = 128
DIFF_V_DIM = 256
CONV_WIDTH = 3
ROPE_THETA = 10000.0
NORM_EPS = 1e-6

LANES = 128
SUBLANES = 8
BF16_ROWS = 16
VMEM_LIMIT_BYTES = 56 * 1024 * 1024

ADA_TN = 1024
PROJ_RET_TM = 512
PROJ_DIFF_TM = 1024
PROJ_TN = 512
RET_KCHUNK = 256
RET_HEADS_PER_STEP = 2
FFN_TM = 512
FFN_TF = 256
ATT_TQ = 256
WEIGHT_STAGE_BYTES = 3 * 1024 * 1024
WEIGHT_STAGE_SLOTS = 2


def _cparams(semantics):
    return pltpu.CompilerParams(dimension_semantics=semantics,
                                vmem_limit_bytes=VMEM_LIMIT_BYTES)


def _rms(x):
    return x * lax.rsqrt(jnp.mean(x * x, axis=-1, keepdims=True) + NORM_EPS)


def _prenorm(x, gain, sc, sh):
    return (_rms(x) * gain) * (1.0 + sc) + sh


def _sigmoid(x):
    return 1.0 / (1.0 + jnp.exp(-x))


def _hbm_weight_spec():
    return pl.BlockSpec(memory_space=pl.ANY)


def _resident_spec(w):
    return pl.BlockSpec(w.shape, lambda *_: (0, 0))


def _cast_job_specs(jobs, n_steps):
    in_specs, out_specs, out_shapes = [], [], []
    for w, layer in jobs:
        _, k, n = w.shape
        reps = next(r for r in (1, 2, 4, 8) if (k * r) % (n_steps * BF16_ROWS) == 0)
        rows = k * reps // n_steps
        in_specs.append(pl.BlockSpec((None, rows, n), lambda i, l=layer, r=reps: (l, i // r, 0)))
        out_specs.append(pl.BlockSpec((rows, n), lambda i, r=reps: (i // r, 0)))
        out_shapes.append(jax.ShapeDtypeStruct((k, n), BF16))
    return in_specs, out_specs, out_shapes


def _run_cast_jobs(in_refs, out_refs):
    for src, dst in zip(in_refs, out_refs):
        dst[...] = src[...].astype(BF16)


def _stage_rows(k, n):
    return max(r for r in range(BF16_ROWS, k + 1, BF16_ROWS)
               if k % r == 0 and r * n * 4 <= WEIGHT_STAGE_BYTES)


def _staging_scratch(weights):
    n = weights[0].shape[2]
    assert all(w.shape[2] == n for w in weights)
    rows = max(_stage_rows(w.shape[1], n) for w in weights)
    return [pltpu.VMEM((WEIGHT_STAGE_SLOTS, rows, n), F32),
            pltpu.SemaphoreType.DMA((WEIGHT_STAGE_SLOTS,))]


def _bf16_weight_scratch(w):
    return pltpu.VMEM(w.shape[1:], BF16)


def _stage_weight_bf16(w_hbm, layer, dst_ref, stage_ref, sem):
    k, n = dst_ref.shape
    slots = stage_ref.shape[0]
    rows = _stage_rows(k, n)
    n_chunks = k // rows

    def chunk_copy(c):
        return pltpu.make_async_copy(w_hbm.at[layer, pl.ds(c * rows, rows), :],
                                     stage_ref.at[c % slots, pl.ds(0, rows), :],
                                     sem.at[c % slots])

    for c in range(min(slots - 1, n_chunks)):
        chunk_copy(c).start()
    for c in range(n_chunks):
        ahead = c + slots - 1
        if ahead < n_chunks:
            chunk_copy(ahead).start()
        chunk_copy(c).wait()
        dst_ref[c * rows:(c + 1) * rows, :] = stage_ref[c % slots, 0:rows, :].astype(BF16)


def _ada_kernel(c_ref, w_ref, b_ref, o_ref):
    c = c_ref[...]
    s = (c * _sigmoid(c)).astype(BF16)
    o_ref[...] = jnp.dot(s, w_ref[...].astype(BF16), preferred_element_type=F32) + b_ref[...]


def _ada(c, w, b, layer):
    bsz, d = c.shape
    n = w.shape[-1]
    return pl.pallas_call(
        _ada_kernel,
        grid=(n // ADA_TN,),
        in_specs=[
            pl.BlockSpec((bsz, d), lambda j: (0, 0)),
            pl.BlockSpec((None, d, ADA_TN), lambda j: (layer, 0, j)),
            pl.BlockSpec((None, 1, ADA_TN), lambda j: (layer, 0, j)),
        ],
        out_specs=pl.BlockSpec((bsz, ADA_TN), lambda j: (0, j)),
        out_shape=jax.ShapeDtypeStruct((bsz, n), F32),
        compiler_params=_cparams(("arbitrary",)),
        name="ada",
    )(c, w, b)


def _rope_ret(acc, lo, cos, sin, pos1, o_ref):
    qk_w = RET_HEADS * RET_QK_DIM
    for hh in range(PROJ_TN // RET_QK_DIM):
        c0 = hh * RET_QK_DIM
        col = lo + c0
        head = (col % qk_w) // RET_QK_DIM
        log_gamma = math.log(1.0 - 2.0 ** (-5.0 - head))
        if col < qk_w:
            dec = jnp.exp(pos1 * log_gamma)
        else:
            dec = jnp.exp(pos1 * (-log_gamma)) * (RET_QK_DIM ** -0.5)
        x1 = acc[:, c0:c0 + LANES]
        x2 = acc[:, c0 + LANES:c0 + 2 * LANES]
        o_ref[:, col:col + LANES] = ((x1 * cos - x2 * sin) * dec).astype(BF16)
        o_ref[:, col + LANES:col + 2 * LANES] = ((x2 * cos + x1 * sin) * dec).astype(BF16)


def _rope_diff(acc, lo, cos, sin, o_ref):
    for bb in range(PROJ_TN // LANES):
        c0 = bb * LANES
        xb = acc[:, c0:c0 + LANES]
        o_ref[:, lo + c0:lo + c0 + LANES] = (
            xb * cos + pltpu.roll(xb, LANES // 2, 1) * sin).astype(BF16)


def _proj_kernel(*refs, n_branch, n_cast, staged, rope_mode, rope_cols, silu_from):
    x_ref = refs[0]
    p = 1 + 4 * n_branch
    cos = refs[p][...]
    sin = refs[p + 1][...]
    cast_in = refs[p + 2:p + 2 + n_cast]
    out_refs = refs[p + 2 + n_cast:p + 2 + n_cast + n_branch]
    cast_out = refs[p + 2 + n_cast + n_branch:p + 2 + 2 * n_cast + n_branch]
    scratch = refs[p + 2 + 2 * n_cast + n_branch:]
    tm = x_ref.shape[0]

    if staged:
        @pl.when(pl.program_id(0) == 0)
        def _():
            for br in range(n_branch):
                _stage_weight_bf16(refs[4 + 4 * br], 0, *scratch[3 * br:3 * br + 3])

    _run_cast_jobs(cast_in, cast_out)

    normed = _rms(x_ref[...])
    if rope_mode == "ret":
        row = lax.broadcasted_iota(jnp.int32, (tm, 1), 0)
        pos1 = (jnp.bitwise_and(row, RET_KCHUNK - 1) + 1).astype(F32)
    for br in range(n_branch):
        gain_ref, sc_ref, sh_ref = refs[1 + 4 * br:4 + 4 * br]
        w_ref = scratch[3 * br] if staged else refs[4 + 4 * br]
        o_ref = out_refs[br]
        h = ((normed * gain_ref[...]) * (1.0 + sc_ref[...]) + sh_ref[...]).astype(BF16)
        for j in range(w_ref.shape[1] // PROJ_TN):
            lo = j * PROJ_TN
            acc = jnp.dot(h, w_ref[:, lo:lo + PROJ_TN], preferred_element_type=F32)
            if lo >= silu_from:
                o_ref[:, lo:lo + PROJ_TN] = (acc * _sigmoid(acc)).astype(BF16)
            elif lo >= rope_cols:
                o_ref[:, lo:lo + PROJ_TN] = acc.astype(BF16)
            elif rope_mode == "ret":
                _rope_ret(acc, lo, cos, sin, pos1, o_ref)
            else:
                _rope_diff(acc, lo, cos, sin, o_ref)


def _proj(x, branches, cos, sin, *, tm, seq, rope_mode, rope_cols, staged, cast_jobs=(),
          silu_from=None):
    t, d = x.shape
    tiles_per_seq = seq // tm
    n_steps = t // tm
    in_specs = [pl.BlockSpec((tm, d), lambda i: (i, 0))]
    args = [x]
    out_specs, out_shapes, scratch = [], [], []
    for gains, gain_idx, mod, sc_idx, sh_idx, w in branches:
        n = w.shape[-1]
        in_specs += [
            pl.BlockSpec((None, 1, d), lambda i, g=gain_idx: (g, 0, 0)),
            pl.BlockSpec((None, 1, d), lambda i, c=sc_idx: (i // tiles_per_seq, 0, c)),
            pl.BlockSpec((None, 1, d), lambda i, c=sh_idx: (i // tiles_per_seq, 0, c)),
            _hbm_weight_spec() if staged else _resident_spec(w),
        ]
        args += [gains, mod, mod, w]
        if staged:
            scratch += [_bf16_weight_scratch(w)] + _staging_scratch([w])
        out_specs.append(pl.BlockSpec((tm, n), lambda i: (i, 0)))
        out_shapes.append(jax.ShapeDtypeStruct((t, n), BF16))
    in_specs += [pl.BlockSpec((tm, LANES), lambda i: (i % tiles_per_seq, 0))] * 2
    args += [cos, sin]
    cast_in, cast_out, cast_shapes = _cast_job_specs(cast_jobs, n_steps)
    in_specs += cast_in
    args += [w for w, _ in cast_jobs]
    if silu_from is None:
        silu_from = max(w.shape[-1] for *_, w in branches)
    kern = functools.partial(_proj_kernel, n_branch=len(branches), n_cast=len(cast_jobs),
                             staged=staged, rope_mode=rope_mode, rope_cols=rope_cols,
                             silu_from=silu_from)
    return pl.pallas_call(
        kern,
        grid=(n_steps,),
        in_specs=in_specs,
        out_specs=out_specs + cast_out,
        out_shape=out_shapes + cast_shapes,
        scratch_shapes=scratch,
        compiler_params=_cparams(("arbitrary",)),
        name="proj_" + rope_mode,
    )(*args)


def _ret_kernel(q_ref, k_ref, v_ref, g_ref, o_ref):
    seq = q_ref.shape[0]
    chunk = RET_KCHUNK
    dk, dv = RET_QK_DIM, RET_V_DIM
    row = lax.broadcasted_iota(jnp.int32, (chunk, chunk), 0)
    col = lax.broadcasted_iota(jnp.int32, (chunk, chunk), 1)
    causal = col <= row
    n_chunks = seq // chunk

    for j in range(RET_HEADS_PER_STEP):
        hf = (pl.program_id(1) * RET_HEADS_PER_STEP + j).astype(F32)
        log_gamma = jnp.log(1.0 - jnp.exp2(jnp.full((1, 1), -5.0, F32) - hf))
        c_dec = jnp.exp(log_gamma * float(chunk))

        def intra(n):
            r0 = n * chunk
            qc = q_ref[r0:r0 + chunk, j * dk:(j + 1) * dk]
            kc = k_ref[r0:r0 + chunk, j * dk:(j + 1) * dk]
            vc = v_ref[r0:r0 + chunk, j * dv:(j + 1) * dv]
            s = lax.dot_general(qc, kc, (((1,), (1,)), ((), ())), preferred_element_type=F32)
            inner = jnp.where(causal, s, 0.0).astype(BF16)
            out = jnp.dot(inner, vc, preferred_element_type=F32)
            kv = None
            if n + 1 < n_chunks:
                kv = lax.dot_general(kc, vc, (((0,), (0,)), ((), ())),
                                     preferred_element_type=F32)
            return out, kv

        state = None
        nxt = intra(0)
        for n in range(n_chunks):
            r0 = n * chunk
            out, kv = nxt
            if n + 1 < n_chunks:
                nxt = intra(n + 1)
            if state is not None:
                out = out + jnp.dot(q_ref[r0:r0 + chunk, j * dk:(j + 1) * dk],
                                    state.astype(BF16), preferred_element_type=F32)
            if kv is not None:
                state = c_dec * (kv if state is None else state + kv)
            gate = g_ref[r0:r0 + chunk, j * dv:(j + 1) * dv].astype(F32)
            o_ref[r0:r0 + chunk, j * dv:(j + 1) * dv] = (gate * _rms(out)).astype(BF16)


def _retention(qkvg, bsz, seq):
    hps = RET_HEADS_PER_STEP
    dk = hps * RET_QK_DIM
    dv = hps * RET_V_DIM
    groups = RET_HEADS // hps
    qk_blocks = RET_HEADS * RET_QK_DIM // dv
    return pl.pallas_call(
        _ret_kernel,
        grid=(bsz, groups),
        in_specs=[
            pl.BlockSpec((None, seq, dk), lambda b, h: (b, 0, h)),
            pl.BlockSpec((None, seq, dk), lambda b, h: (b, 0, groups + h)),
            pl.BlockSpec((None, seq, dv), lambda b, h: (b, 0, 2 * qk_blocks + h)),
            pl.BlockSpec((None, seq, dv), lambda b, h: (b, 0, 2 * qk_blocks + groups + h)),
        ],
        out_specs=pl.BlockSpec((None, seq, dv), lambda b, h: (b, 0, h)),
        out_shape=jax.ShapeDtypeStruct((bsz, seq, RET_HEADS * RET_V_DIM), BF16),
        compiler_params=_cparams(("arbitrary", "arbitrary")),
        name="retention",
    )(qkvg, qkvg, qkvg, qkvg)


def _ffn_kernel(*refs, n_cast, tiles_per_seq, final_norm):
    (mix_ref, x_ref, g1_ref, gain_ref, sc_ref, sh_ref, g2_ref, wout_ref, win_ref,
     wconv_ref, bconv_ref, wdown_ref, fgain_ref) = refs[:13]
    cast_in = refs[13:13 + n_cast]
    o_ref = refs[13 + n_cast]
    cast_out = refs[14 + n_cast:14 + 2 * n_cast]
    carry_ref, y_ref = refs[14 + 2 * n_cast:]
    tm = x_ref.shape[0]
    dff = wdown_ref.shape[0]

    _run_cast_jobs(cast_in, cast_out)

    x = x_ref[...] + g1_ref[...] * jnp.dot(mix_ref[...], wout_ref[...],
                                           preferred_element_type=F32)
    h = _prenorm(x, gain_ref[...], sc_ref[...], sh_ref[...]).astype(BF16)
    seq_start = (pl.program_id(0) % tiles_per_seq) == 0
    sqrt_half = math.sqrt(0.5)

    for f in range(dff // FFN_TF):
        lo = f * FFN_TF
        a = jnp.dot(h, win_ref[:, lo:lo + FFN_TF], preferred_element_type=F32)
        g = jnp.dot(h, win_ref[:, dff + lo:dff + lo + FFN_TF], preferred_element_type=F32)
        w0 = wconv_ref[0:1, lo:lo + FFN_TF]
        w1 = wconv_ref[1:2, lo:lo + FFN_TF]
        w2 = wconv_ref[2:3, lo:lo + FFN_TF]
        bias = bconv_ref[:, lo:lo + FFN_TF]

        def conv_gelu_gate(a0, a1, a2, gate):
            ac = w0 * a2 + w1 * a1 + w2 * a0 + bias
            return (0.5 * ac * (1.0 + lax.erf(ac * sqrt_half))) * gate

        y_ref[:, lo:lo + FFN_TF] = conv_gelu_gate(
            a, pltpu.roll(a, 1, 0), pltpu.roll(a, 2, 0), g).astype(BF16)
        prev = jnp.where(seq_start, 0.0, carry_ref[:, lo:lo + FFN_TF])
        ext = jnp.concatenate([prev, a[0:BF16_ROWS]], axis=0)
        top = conv_gelu_gate(a[0:BF16_ROWS],
                             pltpu.roll(ext, 1, 0)[SUBLANES:],
                             pltpu.roll(ext, 2, 0)[SUBLANES:],
                             g[0:BF16_ROWS])
        y_ref[0:BF16_ROWS, lo:lo + FFN_TF] = top.astype(BF16)
        carry_ref[:, lo:lo + FFN_TF] = a[tm - SUBLANES:tm]

    out = x + g2_ref[...] * jnp.dot(y_ref[...], wdown_ref[...], preferred_element_type=F32)
    if final_norm:
        out = _rms(out) * fgain_ref[...]
    o_ref[...] = out


def _ffn(mix, x, gains, gain_idx, mod, w_out, w_in, w_conv, b_conv, w_down, fgain, layer, *,
         seq, final_norm, cast_jobs=()):
    t, d = x.shape
    kmix = mix.shape[1]
    dff = w_down.shape[0]
    tiles_per_seq = seq // FFN_TM
    n_steps = t // FFN_TM
    cast_in, cast_out, cast_shapes = _cast_job_specs(cast_jobs, n_steps)
    kern = functools.partial(_ffn_kernel, n_cast=len(cast_jobs), tiles_per_seq=tiles_per_seq,
                             final_norm=final_norm)
    return pl.pallas_call(
        kern,
        grid=(n_steps,),
        in_specs=[
            pl.BlockSpec((FFN_TM, kmix), lambda i: (i, 0)),
            pl.BlockSpec((FFN_TM, d), lambda i: (i, 0)),
            pl.BlockSpec((None, 1, d), lambda i: (i // tiles_per_seq, 0, 2)),
            pl.BlockSpec((None, 1, d), lambda i: (gain_idx, 0, 0)),
            pl.BlockSpec((None, 1, d), lambda i: (i // tiles_per_seq, 0, 4)),
            pl.BlockSpec((None, 1, d), lambda i: (i // tiles_per_seq, 0, 3)),
            pl.BlockSpec((None, 1, d), lambda i: (i // tiles_per_seq, 0, 5)),
            _resident_spec(w_out),
            _resident_spec(w_in),
            pl.BlockSpec((None, CONV_WIDTH, dff), lambda i: (layer, 0, 0)),
            pl.BlockSpec((None, 1, dff), lambda i: (layer, 0, 0)),
            _resident_spec(w_down),
            pl.BlockSpec((1, d), lambda i: (0, 0)),
        ] + cast_in,
        out_specs=[pl.BlockSpec((FFN_TM, d), lambda i: (i, 0))] + cast_out,
        out_shape=[jax.ShapeDtypeStruct((t, d), F32)] + cast_shapes,
        scratch_shapes=[
            pltpu.VMEM((SUBLANES, dff), F32),
            pltpu.VMEM((FFN_TM, dff), BF16),
        ],
        compiler_params=_cparams(("arbitrary",)),
        name="convffn",
    )(mix, x, mod, gains, mod, mod, mod, w_out, w_in, w_conv, b_conv, w_down, fgain,
      *[w for w, _ in cast_jobs])


def _attn_kernel(lam_ref, q_ref, k_ref, v_ref, gain_ref, o_ref, *, lam_init):
    seq = q_ref.shape[0]
    tq = ATT_TQ
    hd = DIFF_HEAD_DIM
    exp_scale = (hd ** -0.5) * math.log2(math.e)
    contract_last = (((1,), (1,)), ((), ()))

    lv = lam_ref[...]
    lam = (jnp.exp(jnp.sum(lv[0:1] * lv[1:2], axis=1, keepdims=True))
           - jnp.exp(jnp.sum(lv[2:3] * lv[3:4], axis=1, keepdims=True)) + lam_init)

    row = lax.broadcasted_iota(jnp.int32, (tq, tq), 0)
    col = lax.broadcasted_iota(jnp.int32, (tq, tq), 1)
    causal = col <= row

    items = [(qi, half) for qi in reversed(range(seq // tq)) for half in range(2)]
    scores, probs, outs = {}, {}, {}

    def stage_scores(w):
        qi, half = items[w]
        r0, c0 = qi * tq, half * hd
        qh = q_ref[r0:r0 + tq, c0:c0 + hd]
        s_diag = lax.dot_general(qh, k_ref[r0:r0 + tq, c0:c0 + hd], contract_last,
                                 preferred_element_type=F32)
        s_diag = jnp.where(causal, s_diag, -jnp.inf)
        m = jnp.max(s_diag, axis=-1, keepdims=True)
        s_main = None
        if qi > 0:
            s_main = lax.dot_general(qh, k_ref[0:r0, c0:c0 + hd], contract_last,
                                     preferred_element_type=F32)
            m = jnp.maximum(m, jnp.max(s_main, axis=-1, keepdims=True))
        scores[w] = (s_diag, s_main, m)

    def stage_probs(w):
        s_diag, s_main, m = scores.pop(w)
        p_diag = jnp.exp2((s_diag - m) * exp_scale)
        l = jnp.sum(p_diag, axis=-1, keepdims=True)
        p_main = None
        if s_main is not None:
            p_main = jnp.exp2((s_main - m) * exp_scale)
            l = l + jnp.sum(p_main, axis=-1, keepdims=True)
            p_main = p_main.astype(BF16)
        probs[w] = (p_diag.astype(BF16), p_main, l)

    def stage_values(w):
        qi, half = items[w]
        r0 = qi * tq
        p_diag, p_main, l = probs.pop(w)
        acc = jnp.dot(p_diag, v_ref[r0:r0 + tq, :], preferred_element_type=F32)
        if p_main is not None:
            acc = acc + jnp.dot(p_main, v_ref[0:r0, :], preferred_element_type=F32)
        outs[w] = acc * (1.0 / l)
        if half == 1:
            o = outs.pop(w - 1) - lam * outs.pop(w)
            o_ref[r0:r0 + tq, :] = ((_rms(o) * gain_ref[...]) * (1.0 - lam_init)).astype(BF16)

    n_q = len(items) // 2
    for t in range(n_q + 2):
        for half in range(2):
            if t < n_q:
                stage_scores(2 * t + half)
        for half in range(2):
            if 0 <= t - 1 < n_q:
                stage_probs(2 * (t - 1) + half)
        for half in range(2):
            if 0 <= t - 2 < n_q:
                stage_values(2 * (t - 2) + half)


def _attention(q, kv, lam_params, subln_gain, *, lam_init):
    bsz, seq, _ = q.shape
    w = 2 * DIFF_HEAD_DIM
    kern = functools.partial(_attn_kernel, lam_init=lam_init)
    return pl.pallas_call(
        kern,
        grid=(bsz, DIFF_HEADS),
        in_specs=[
            pl.BlockSpec((4, DIFF_HEAD_DIM), lambda b, h: (0, 0)),
            pl.BlockSpec((None, seq, w), lambda b, h: (b, 0, h)),
            pl.BlockSpec((None, seq, w), lambda b, h: (b, 0, h)),
            pl.BlockSpec((None, seq, DIFF_V_DIM), lambda b, h: (b, 0, DIFF_HEADS + h)),
            pl.BlockSpec((1, DIFF_V_DIM), lambda b, h: (0, 0)),
        ],
        out_specs=pl.BlockSpec((None, seq, DIFF_V_DIM), lambda b, h: (b, 0, h)),
        out_shape=jax.ShapeDtypeStruct((bsz, seq, DIFF_HEADS * DIFF_V_DIM), BF16),
        compiler_params=_cparams(("arbitrary", "arbitrary")),
        name="diffattn",
    )(lam_params, q, kv, kv, subln_gain)


def kernel(x, c, norm_gain, w_ada, b_ada, ret_w_in, ret_w_out, ffn_w_in, ffn_w_conv,
           ffn_b_conv, ffn_w_down, kv_norm_gain, kv_w_ada, kv_b_ada, w_kv, diff_w_q,
           diff_lambda, diff_subln_gain, diff_w_out, final_norm_gain):
    bsz, seq, d = x.shape
    t = bsz * seq
    depth = w_ada.shape[0]
    n_a = ret_w_in.shape[0]
    assert depth == 2 and n_a == 1 and diff_w_q.shape[0] == 1
    assert seq % PROJ_RET_TM == 0 and seq % PROJ_DIFF_TM == 0 and seq % FFN_TM == 0
    assert seq % ATT_TQ == 0 and PROJ_RET_TM % RET_KCHUNK == 0
    assert RET_KCHUNK & (RET_KCHUNK - 1) == 0

    pos = jnp.arange(seq, dtype=F32)[:, None]
    ret_freqs = 1.0 / (ROPE_THETA ** jnp.linspace(0.0, 1.0, LANES))
    ret_ang = pos * ret_freqs[None, :]
    ret_cos, ret_sin = jnp.cos(ret_ang), jnp.sin(ret_ang)
    diff_freqs = 1.0 / (ROPE_THETA ** (jnp.arange(0, DIFF_HEAD_DIM, 2, dtype=F32) / DIFF_HEAD_DIM))
    diff_ang = pos * diff_freqs[None, :]
    dcos, dsin = jnp.cos(diff_ang), jnp.sin(diff_ang)
    diff_cos = jnp.concatenate([dcos, dcos], axis=1)
    diff_sin = jnp.concatenate([-dsin, dsin], axis=1)

    xf = x.reshape(t, d)
    gains = norm_gain.reshape(depth * 2, 1, d)
    b_ada3 = b_ada.reshape(depth, 1, 6 * d)
    ffn_b3 = ffn_b_conv.reshape(depth, 1, -1)
    mod0 = _ada(c, w_ada, b_ada3, 0).reshape(bsz, 1, 6 * d)
    mod1 = _ada(c, w_ada, b_ada3, 1).reshape(bsz, 1, 6 * d)
    kvmod = _ada(c, kv_w_ada[None], kv_b_ada.reshape(1, 1, 2 * d), 0).reshape(bsz, 1, 2 * d)

    qk_cols = 2 * RET_HEADS * RET_QK_DIM
    qkvg, w_out0, w_in0, w_down0 = _proj(
        xf, [(gains, 0, mod0, 1, 0, ret_w_in)], ret_cos, ret_sin, tm=PROJ_RET_TM, seq=seq,
        rope_mode="ret", rope_cols=qk_cols, silu_from=qk_cols + RET_HEADS * RET_V_DIM,
        staged=True, cast_jobs=[(ret_w_out, 0), (ffn_w_in, 0), (ffn_w_down, 0)])
    y = _retention(qkvg.reshape(bsz, seq, -1), bsz, seq).reshape(t, -1)
    xf, w_kv_bf, w_q_bf, w_out1, w_in1, w_down1 = _ffn(
        y, xf, gains, 1, mod0, w_out0, w_in0, ffn_w_conv, ffn_b3, w_down0,
        final_norm_gain[None], 0, seq=seq, final_norm=False,
        cast_jobs=[(w_kv[None], 0), (diff_w_q, 0), (diff_w_out, 0), (ffn_w_in, 1),
                   (ffn_w_down, 1)])

    k_dim = DIFF_HEADS * 2 * DIFF_HEAD_DIM
    kv, q = _proj(xf, [(kv_norm_gain.reshape(1, 1, d), 0, kvmod, 1, 0, w_kv_bf),
                       (gains, 2, mod1, 1, 0, w_q_bf)],
                  diff_cos, diff_sin, tm=PROJ_DIFF_TM, seq=seq, rope_mode="diff",
                  rope_cols=k_dim, staged=False)

    lam_init = 0.8 - 0.6 * math.exp(-0.3 * 1)
    o = _attention(q.reshape(bsz, seq, -1), kv.reshape(bsz, seq, -1), diff_lambda[0],
                   diff_subln_gain[0][None], lam_init=lam_init).reshape(t, -1)
    (xf,) = _ffn(o, xf, gains, 3, mod1, w_out1, w_in1, ffn_w_conv, ffn_b3, w_down1,
                 final_norm_gain[None], 1, seq=seq, final_norm=True)
    return xf.reshape(bsz, seq, d)
```

```python
import functools
import math

import jax
import jax.numpy as jnp
from jax import lax
from jax.experimental import pallas as pl
from jax.experimental.pallas import tpu as pltpu

F32 = jnp.float32
BF16 = jnp.bfloat16

RET_HEADS = 4
RET_QK_DIM = 256
RET_V_DIM = 512
DIFF_HEADS = 4
DIFF_HEAD_DIM = 128
DIFF_V_DIM = 256
CONV_WIDTH = 3
ROPE_THETA = 10000.0
NORM_EPS = 1e-6

LANES = 128
SUBLANES = 8
BF16_ROWS = 16
VMEM_LIMIT_BYTES = 56 * 1024 * 1024

ADA_TN = 1024
PROJ_RET_TM = 512
PROJ_DIFF_TM = 1024
PROJ_TN = 512
RET_KCHUNK = 256
RET_HEADS_PER_STEP = 2
FFN_TM = (512, 1024)
FFN_TF = 256
ATT_TQ = 256
WEIGHT_STAGE_BYTES = 3 * 1024 * 1024
WEIGHT_STAGE_SLOTS = 2


def _cparams(semantics):
    return pltpu.CompilerParams(dimension_semantics=semantics,
                                vmem_limit_bytes=VMEM_LIMIT_BYTES)


def _rms(x):
    return x * lax.rsqrt(jnp.mean(x * x, axis=-1, keepdims=True) + NORM_EPS)


def _prenorm(x, gain, sc, sh):
    return (_rms(x) * gain) * (1.0 + sc) + sh


def _sigmoid(x):
    return 1.0 / (1.0 + jnp.exp(-x))


def _hbm_weight_spec():
    return pl.BlockSpec(memory_space=pl.ANY)


def _resident_spec(w):
    return pl.BlockSpec(w.shape, lambda *_: (0, 0))


def _cast_job_specs(jobs, n_steps):
    in_specs, out_specs, out_shapes = [], [], []
    for w, layer in jobs:
        _, k, n = w.shape
        reps = next(r for r in (1, 2, 4, 8) if (k * r) % (n_steps * BF16_ROWS) == 0)
        rows = k * reps // n_steps
        in_specs.append(pl.BlockSpec((None, rows, n), lambda i, l=layer, r=reps: (l, i // r, 0)))
        out_specs.append(pl.BlockSpec((rows, n), lambda i, r=reps: (i // r, 0)))
        out_shapes.append(jax.ShapeDtypeStruct((k, n), BF16))
    return in_specs, out_specs, out_shapes


def _run_cast_jobs(in_refs, out_refs):
    for src, dst in zip(in_refs, out_refs):
        dst[...] = src[...].astype(BF16)


def _stage_rows(k, n):
    return max(r for r in range(BF16_ROWS, k + 1, BF16_ROWS)
               if k % r == 0 and r * n * 4 <= WEIGHT_STAGE_BYTES)


def _staging_scratch(weights):
    n = weights[0].shape[2]
    assert all(w.shape[2] == n for w in weights)
    rows = max(_stage_rows(w.shape[1], n) for w in weights)
    return [pltpu.VMEM((WEIGHT_STAGE_SLOTS, rows, n), F32),
            pltpu.SemaphoreType.DMA((WEIGHT_STAGE_SLOTS,))]


def _bf16_weight_scratch(w):
    return pltpu.VMEM(w.shape[1:], BF16)


def _stage_weight_bf16(w_hbm, layer, dst_ref, stage_ref, sem):
    k, n = dst_ref.shape
    slots = stage_ref.shape[0]
    rows = _stage_rows(k, n)
    n_chunks = k // rows

    def chunk_copy(c):
        return pltpu.make_async_copy(w_hbm.at[layer, pl.ds(c * rows, rows), :],
                                     stage_ref.at[c % slots, pl.ds(0, rows), :],
                                     sem.at[c % slots])

    for c in range(min(slots - 1, n_chunks)):
        chunk_copy(c).start()
    for c in range(n_chunks):
        ahead = c + slots - 1
        if ahead < n_chunks:
            chunk_copy(ahead).start()
        chunk_copy(c).wait()
        dst_ref[c * rows:(c + 1) * rows, :] = stage_ref[c % slots, 0:rows, :].astype(BF16)


def _ada_kernel(c_ref, w_ref, b_ref, o_ref):
    c = c_ref[...]
    s = (c * _sigmoid(c)).astype(BF16)
    o_ref[...] = jnp.dot(s, w_ref[...].astype(BF16), preferred_element_type=F32) + b_ref[...]


def _ada(c, w, b, layer):
    bsz, d = c.shape
    n = w.shape[-1]
    return pl.pallas_call(
        _ada_kernel,
        grid=(n // ADA_TN,),
        in_specs=[
            pl.BlockSpec((bsz, d), lambda j: (0, 0)),
            pl.BlockSpec((None, d, ADA_TN), lambda j: (layer, 0, j)),
            pl.BlockSpec((None, 1, ADA_TN), lambda j: (layer, 0, j)),
        ],
        out_specs=pl.BlockSpec((bsz, ADA_TN), lambda j: (0, j)),
        out_shape=jax.ShapeDtypeStruct((bsz, n), F32),
        compiler_params=_cparams(("arbitrary",)),
        name="ada",
    )(c, w, b)


def _rope_ret(acc, lo, cos, sin, pos1, o_ref):
    qk_w = RET_HEADS * RET_QK_DIM
    for hh in range(PROJ_TN // RET_QK_DIM):
        c0 = hh * RET_QK_DIM
        col = lo + c0
        head = (col % qk_w) // RET_QK_DIM
        log_gamma = math.log(1.0 - 2.0 ** (-5.0 - head))
        if col < qk_w:
            dec = jnp.exp(pos1 * log_gamma)
        else:
            dec = jnp.exp(pos1 * (-log_gamma)) * (RET_QK_DIM ** -0.5)
        x1 = acc[:, c0:c0 + LANES]
        x2 = acc[:, c0 + LANES:c0 + 2 * LANES]
        o_ref[:, col:col + LANES] = ((x1 * cos - x2 * sin) * dec).astype(BF16)
        o_ref[:, col + LANES:col + 2 * LANES] = ((x2 * cos + x1 * sin) * dec).astype(BF16)


def _rope_diff(acc, lo, cos, sin, o_ref):
    for bb in range(PROJ_TN // LANES):
        c0 = bb * LANES
        xb = acc[:, c0:c0 + LANES]
        o_ref[:, lo + c0:lo + c0 + LANES] = (
            xb * cos + pltpu.roll(xb, LANES // 2, 1) * sin).astype(BF16)


def _proj_kernel(*refs, n_branch, n_cast, staged, rope_mode, rope_cols, silu_from):
    x_ref = refs[0]
    p = 1 + 4 * n_branch
    cos = refs[p][...]
    sin = refs[p + 1][...]
    cast_in = refs[p + 2:p + 2 + n_cast]
    out_refs = refs[p + 2 + n_cast:p + 2 + n_cast + n_branch]
    cast_out = refs[p + 2 + n_cast + n_branch:p + 2 + 2 * n_cast + n_branch]
    scratch = refs[p + 2 + 2 * n_cast + n_branch:]
    tm = x_ref.shape[0]

    if staged:
        @pl.when(pl.program_id(0) == 0)
        def _():
            for br in range(n_branch):
                _stage_weight_bf16(refs[4 + 4 * br], 0, *scratch[3 * br:3 * br + 3])

    _run_cast_jobs(cast_in, cast_out)

    normed = _rms(x_ref[...])
    if rope_mode == "ret":
        row = lax.broadcasted_iota(jnp.int32, (tm, 1), 0)
        pos1 = (jnp.bitwise_and(row, RET_KCHUNK - 1) + 1).astype(F32)
    for br in range(n_branch):
        gain_ref, sc_ref, sh_ref = refs[1 + 4 * br:4 + 4 * br]
        w_ref = scratch[3 * br] if staged else refs[4 + 4 * br]
        o_ref = out_refs[br]
        h = ((normed * gain_ref[...]) * (1.0 + sc_ref[...]) + sh_ref[...]).astype(BF16)
        for j in range(w_ref.shape[1] // PROJ_TN):
            lo = j * PROJ_TN
            acc = jnp.dot(h, w_ref[:, lo:lo + PROJ_TN], preferred_element_type=F32)
            if lo >= silu_from:
                o_ref[:, lo:lo + PROJ_TN] = (acc * _sigmoid(acc)).astype(BF16)
            elif lo >= rope_cols:
                o_ref[:, lo:lo + PROJ_TN] = acc.astype(BF16)
            elif rope_mode == "ret":
                _rope_ret(acc, lo, cos, sin, pos1, o_ref)
            else:
                _rope_diff(acc, lo, cos, sin, o_ref)


def _proj(x, branches, cos, sin, *, tm, seq, rope_mode, rope_cols, staged, cast_jobs=(),
          silu_from=None):
    t, d = x.shape
    tiles_per_seq = seq // tm
    n_steps = t // tm
    in_specs = [pl.BlockSpec((tm, d), lambda i: (i, 0))]
    args = [x]
    out_specs, out_shapes, scratch = [], [], []
    for gains, gain_idx, mod, sc_idx, sh_idx, w in branches:
        n = w.shape[-1]
        in_specs += [
            pl.BlockSpec((None, 1, d), lambda i, g=gain_idx: (g, 0, 0)),
            pl.BlockSpec((None, 1, d), lambda i, c=sc_idx: (i // tiles_per_seq, 0, c)),
            pl.BlockSpec((None, 1, d), lambda i, c=sh_idx: (i // tiles_per_seq, 0, c)),
            _hbm_weight_spec() if staged else _resident_spec(w),
        ]
        args += [gains, mod, mod, w]
        if staged:
            scratch += [_bf16_weight_scratch(w)] + _staging_scratch([w])
        out_specs.append(pl.BlockSpec((tm, n), lambda i: (i, 0)))
        out_shapes.append(jax.ShapeDtypeStruct((t, n), BF16))
    in_specs += [pl.BlockSpec((tm, LANES), lambda i: (i % tiles_per_seq, 0))] * 2
    args += [cos, sin]
    cast_in, cast_out, cast_shapes = _cast_job_specs(cast_jobs, n_steps)
    in_specs += cast_in
    args += [w for w, _ in cast_jobs]
    if silu_from is None:
        silu_from = max(w.shape[-1] for *_, w in branches)
    kern = functools.partial(_proj_kernel, n_branch=len(branches), n_cast=len(cast_jobs),
                             staged=staged, rope_mode=rope_mode, rope_cols=rope_cols,
                             silu_from=silu_from)
    return pl.pallas_call(
        kern,
        grid=(n_steps,),
        in_specs=in_specs,
        out_specs=out_specs + cast_out,
        out_shape=out_shapes + cast_shapes,
        scratch_shapes=scratch,
        compiler_params=_cparams(("arbitrary",)),
        name="proj_" + rope_mode,
    )(*args)


def _ret_kernel(q_ref, k_ref, v_ref, g_ref, o_ref):
    seq = q_ref.shape[0]
    chunk = RET_KCHUNK
    dk, dv = RET_QK_DIM, RET_V_DIM
    row = lax.broadcasted_iota(jnp.int32, (chunk, chunk), 0)
    col = lax.broadcasted_iota(jnp.int32, (chunk, chunk), 1)
    causal = col <= row
    n_chunks = seq // chunk

    for j in range(RET_HEADS_PER_STEP):
        hf = (pl.program_id(1) * RET_HEADS_PER_STEP + j).astype(F32)
        log_gamma = jnp.log(1.0 - jnp.exp2(jnp.full((1, 1), -5.0, F32) - hf))
        c_dec = jnp.exp(log_gamma * float(chunk))

        def intra(n):
            r0 = n * chunk
            qc = q_ref[r0:r0 + chunk, j * dk:(j + 1) * dk]
            kc = k_ref[r0:r0 + chunk, j * dk:(j + 1) * dk]
            vc = v_ref[r0:r0 + chunk, j * dv:(j + 1) * dv]
            s = lax.dot_general(qc, kc, (((1,), (1,)), ((), ())), preferred_element_type=F32)
            inner = jnp.where(causal, s, 0.0).astype(BF16)
            out = jnp.dot(inner, vc, preferred_element_type=F32)
            kv = None
            if n + 1 < n_chunks:
                kv = lax.dot_general(kc, vc, (((0,), (0,)), ((), ())),
                                     preferred_element_type=F32)
            return out, kv

        state = None
        nxt = intra(0)
        for n in range(n_chunks):
            r0 = n * chunk
            out, kv = nxt
            if n + 1 < n_chunks:
                nxt = intra(n + 1)
            if state is not None:
                out = out + jnp.dot(q_ref[r0:r0 + chunk, j * dk:(j + 1) * dk],
                                    state.astype(BF16), preferred_element_type=F32)
            if kv is not None:
                state = c_dec * (kv if state is None else state + kv)
            gate = g_ref[r0:r0 + chunk, j * dv:(j + 1) * dv].astype(F32)
            o_ref[r0:r0 + chunk, j * dv:(j + 1) * dv] = (gate * _rms(out)).astype(BF16)


def _retention(qkvg, bsz, seq):
    hps = RET_HEADS_PER_STEP
    dk = hps * RET_QK_DIM
    dv = hps * RET_V_DIM
    groups = RET_HEADS // hps
    qk_blocks = RET_HEADS * RET_QK_DIM // dv
    return pl.pallas_call(
        _ret_kernel,
        grid=(bsz, groups),
        in_specs=[
            pl.BlockSpec((None, seq, dk), lambda b, h: (b, 0, h)),
            pl.BlockSpec((None, seq, dk), lambda b, h: (b, 0, groups + h)),
            pl.BlockSpec((None, seq, dv), lambda b, h: (b, 0, 2 * qk_blocks + h)),
            pl.BlockSpec((None, seq, dv), lambda b, h: (b, 0, 2 * qk_blocks + groups + h)),
        ],
        out_specs=pl.BlockSpec((None, seq, dv), lambda b, h: (b, 0, h)),
        out_shape=jax.ShapeDtypeStruct((bsz, seq, RET_HEADS * RET_V_DIM), BF16),
        compiler_params=_cparams(("arbitrary", "arbitrary")),
        name="retention",
    )(qkvg, qkvg, qkvg, qkvg)


def _ffn_kernel(*refs, n_cast, tiles_per_seq, final_norm):
    (mix_ref, x_ref, g1_ref, gain_ref, sc_ref, sh_ref, g2_ref, wout_ref, win_ref,
     wconv_ref, bconv_ref, wdown_ref, fgain_ref) = refs[:13]
    cast_in = refs[13:13 + n_cast]
    o_ref = refs[13 + n_cast]
    cast_out = refs[14 + n_cast:14 + 2 * n_cast]
    carry_ref, y_ref = refs[14 + 2 * n_cast:]
    tm = x_ref.shape[0]
    dff = wdown_ref.shape[0]

    _run_cast_jobs(cast_in, cast_out)

    x = x_ref[...] + g1_ref[...] * jnp.dot(mix_ref[...], wout_ref[...],
                                           preferred_element_type=F32)
    h = _prenorm(x, gain_ref[...], sc_ref[...], sh_ref[...]).astype(BF16)
    seq_start = (pl.program_id(0) % tiles_per_seq) == 0
    sqrt_half = math.sqrt(0.5)

    for f in range(dff // FFN_TF):
        lo = f * FFN_TF
        a = jnp.dot(h, win_ref[:, lo:lo + FFN_TF], preferred_element_type=F32)
        g = jnp.dot(h, win_ref[:, dff + lo:dff + lo + FFN_TF], preferred_element_type=F32)
        w0 = wconv_ref[0:1, lo:lo + FFN_TF]
        w1 = wconv_ref[1:2, lo:lo + FFN_TF]
        w2 = wconv_ref[2:3, lo:lo + FFN_TF]
        bias = bconv_ref[:, lo:lo + FFN_TF]

        def conv_gelu_gate(a0, a1, a2, gate):
            ac = w0 * a2 + w1 * a1 + w2 * a0 + bias
            return (0.5 * ac * (1.0 + lax.erf(ac * sqrt_half))) * gate

        y_ref[:, lo:lo + FFN_TF] = conv_gelu_gate(
            a, pltpu.roll(a, 1, 0), pltpu.roll(a, 2, 0), g).astype(BF16)
        prev = jnp.where(seq_start, 0.0, carry_ref[:, lo:lo + FFN_TF])
        ext = jnp.concatenate([prev, a[0:BF16_ROWS]], axis=0)
        top = conv_gelu_gate(a[0:BF16_ROWS],
                             pltpu.roll(ext, 1, 0)[SUBLANES:],
                             pltpu.roll(ext, 2, 0)[SUBLANES:],
                             g[0:BF16_ROWS])
        y_ref[0:BF16_ROWS, lo:lo + FFN_TF] = top.astype(BF16)
        carry_ref[:, lo:lo + FFN_TF] = a[tm - SUBLANES:tm]

    out = x + g2_ref[...] * jnp.dot(y_ref[...], wdown_ref[...], preferred_element_type=F32)
    if final_norm:
        out = _rms(out) * fgain_ref[...]
    o_ref[...] = out


def _ffn(mix, x, gains, gain_idx, mod, w_out, w_in, w_conv, b_conv, w_down, fgain, layer, *,
         seq, final_norm, cast_jobs=()):
    t, d = x.shape
    kmix = mix.shape[1]
    dff = w_down.shape[0]
    tm = FFN_TM[layer]
    tiles_per_seq = seq // tm
    n_steps = t // tm
    cast_in, cast_out, cast_shapes = _cast_job_specs(cast_jobs, n_steps)
    kern = functools.partial(_ffn_kernel, n_cast=len(cast_jobs), tiles_per_seq=tiles_per_seq,
                             final_norm=final_norm)
    return pl.pallas_call(
        kern,
        grid=(n_steps,),
        in_specs=[
            pl.BlockSpec((tm, kmix), lambda i: (i, 0)),
            pl.BlockSpec((tm, d), lambda i: (i, 0)),
            pl.BlockSpec((None, 1, d), lambda i: (i // tiles_per_seq, 0, 2)),
            pl.BlockSpec((None, 1, d), lambda i: (gain_idx, 0, 0)),
            pl.BlockSpec((None, 1, d), lambda i: (i // tiles_per_seq, 0, 4)),
            pl.BlockSpec((None, 1, d), lambda i: (i // tiles_per_seq, 0, 3)),
            pl.BlockSpec((None, 1, d), lambda i: (i // tiles_per_seq, 0, 5)),
            _resident_spec(w_out),
            _resident_spec(w_in),
            pl.BlockSpec((None, CONV_WIDTH, dff), lambda i: (layer, 0, 0)),
            pl.BlockSpec((None, 1, dff), lambda i: (layer, 0, 0)),
            _resident_spec(w_down),
            pl.BlockSpec((1, d), lambda i: (0, 0)),
        ] + cast_in,
        out_specs=[pl.BlockSpec((tm, d), lambda i: (i, 0))] + cast_out,
        out_shape=[jax.ShapeDtypeStruct((t, d), F32)] + cast_shapes,
        scratch_shapes=[
            pltpu.VMEM((SUBLANES, dff), F32),
            pltpu.VMEM((tm, dff), BF16),
        ],
        compiler_params=_cparams(("arbitrary",)),
        name="convffn",
    )(mix, x, mod, gains, mod, mod, mod, w_out, w_in, w_conv, b_conv, w_down, fgain,
      *[w for w, _ in cast_jobs])


def _attn_kernel(lam_ref, q_ref, k_ref, v_ref, gain_ref, o_ref, *, lam_init):
    seq = q_ref.shape[0]
    tq = ATT_TQ
    hd = DIFF_HEAD_DIM
    exp_scale = (hd ** -0.5) * math.log2(math.e)
    contract_last = (((1,), (1,)), ((), ()))

    lv = lam_ref[...]
    lam = (jnp.exp(jnp.sum(lv[0:1] * lv[1:2], axis=1, keepdims=True))
           - jnp.exp(jnp.sum(lv[2:3] * lv[3:4], axis=1, keepdims=True)) + lam_init)

    row = lax.broadcasted_iota(jnp.int32, (tq, tq), 0)
    col = lax.broadcasted_iota(jnp.int32, (tq, tq), 1)
    causal = col <= row

    items = [(qi, half) for qi in reversed(range(seq // tq)) for half in range(2)]
    scores, probs, outs = {}, {}, {}

    def stage_scores(w):
        qi, half = items[w]
        r0, c0 = qi * tq, half * hd
        qh = q_ref[r0:r0 + tq, c0:c0 + hd]
        s_main = None
        if qi > 0:
            s_main = lax.dot_general(qh, k_ref[0:r0, c0:c0 + hd], contract_last,
                                     preferred_element_type=F32)
        s_diag = lax.dot_general(qh, k_ref[r0:r0 + tq, c0:c0 + hd], contract_last,
                                 preferred_element_type=F32)
        s_diag = jnp.where(causal, s_diag, -jnp.inf)
        m = jnp.max(s_diag, axis=-1, keepdims=True)
        if qi > 0:
            m = jnp.maximum(m, jnp.max(s_main, axis=-1, keepdims=True))
        scores[w] = (s_diag, s_main, m)

    def stage_probs(w):
        s_diag, s_main, m = scores.pop(w)
        p_diag = jnp.exp2((s_diag - m) * exp_scale)
        l = jnp.sum(p_diag, axis=-1, keepdims=True)
        p_main = None
        if s_main is not None:
            p_main = jnp.exp2((s_main - m) * exp_scale)
            l = l + jnp.sum(p_main, axis=-1, keepdims=True)
            p_main = p_main.astype(BF16)
        probs[w] = (p_diag.astype(BF16), p_main, l)

    def stage_values(w):
        qi, half = items[w]
        r0 = qi * tq
        p_diag, p_main, l = probs.pop(w)
        acc = jnp.dot(p_diag, v_ref[r0:r0 + tq, :], preferred_element_type=F32)
        if p_main is not None:
            acc = acc + jnp.dot(p_main, v_ref[0:r0, :], preferred_element_type=F32)
        outs[w] = acc * (1.0 / l)
        if half == 1:
            o = outs.pop(w - 1) - lam * outs.pop(w)
            o_ref[r0:r0 + tq, :] = ((_rms(o) * gain_ref[...]) * (1.0 - lam_init)).astype(BF16)

    n_q = len(items) // 2
    for t in range(n_q + 2):
        for half in range(2):
            if t < n_q:
                stage_scores(2 * t + half)
        for half in range(2):
            if 0 <= t - 1 < n_q:
                stage_probs(2 * (t - 1) + half)
        for half in range(2):
            if 0 <= t - 2 < n_q:
                stage_values(2 * (t - 2) + half)


def _attention(q, kv, lam_params, subln_gain, *, lam_init):
    bsz, seq, _ = q.shape
    w = 2 * DIFF_HEAD_DIM
    kern = functools.partial(_attn_kernel, lam_init=lam_init)
    return pl.pallas_call(
        kern,
        grid=(bsz, DIFF_HEADS),
        in_specs=[
            pl.BlockSpec((4, DIFF_HEAD_DIM), lambda b, h: (0, 0)),
            pl.BlockSpec((None, seq, w), lambda b, h: (b, 0, h)),
            pl.BlockSpec((None, seq, w), lambda b, h: (b, 0, h)),
            pl.BlockSpec((None, seq, DIFF_V_DIM), lambda b, h: (b, 0, DIFF_HEADS + h)),
            pl.BlockSpec((1, DIFF_V_DIM), lambda b, h: (0, 0)),
        ],
        out_specs=pl.BlockSpec((None, seq, DIFF_V_DIM), lambda b, h: (b, 0, h)),
        out_shape=jax.ShapeDtypeStruct((bsz, seq, DIFF_HEADS * DIFF_V_DIM), BF16),
        compiler_params=_cparams(("arbitrary", "arbitrary")),
        name="diffattn",
    )(lam_params, q, kv, kv, subln_gain)


def kernel(x, c, norm_gain, w_ada, b_ada, ret_w_in, ret_w_out, ffn_w_in, ffn_w_conv,
           ffn_b_conv, ffn_w_down, kv_norm_gain, kv_w_ada, kv_b_ada, w_kv, diff_w_q,
           diff_lambda, diff_subln_gain, diff_w_out, final_norm_gain):
    bsz, seq, d = x.shape
    t = bsz * seq
    depth = w_ada.shape[0]
    n_a = ret_w_in.shape[0]
    assert depth == 2 and n_a == 1 and diff_w_q.shape[0] == 1
    assert seq % PROJ_RET_TM == 0 and seq % PROJ_DIFF_TM == 0 and all(seq % m == 0 for m in FFN_TM)
    assert seq % ATT_TQ == 0 and PROJ_RET_TM % RET_KCHUNK == 0
    assert RET_KCHUNK & (RET_KCHUNK - 1) == 0

    pos = jnp.arange(seq, dtype=F32)[:, None]
    ret_freqs = 1.0 / (ROPE_THETA ** jnp.linspace(0.0, 1.0, LANES))
    ret_ang = pos * ret_freqs[None, :]
    ret_cos, ret_sin = jnp.cos(ret_ang), jnp.sin(ret_ang)
    diff_freqs = 1.0 / (ROPE_THETA ** (jnp.arange(0, DIFF_HEAD_DIM, 2, dtype=F32) / DIFF_HEAD_DIM))
    diff_ang = pos * diff_freqs[None, :]
    dcos, dsin = jnp.cos(diff_ang), jnp.sin(diff_ang)
    diff_cos = jnp.concatenate([dcos, dcos], axis=1)
    diff_sin = jnp.concatenate([-dsin, dsin], axis=1)

    xf = x.reshape(t, d)
    gains = norm_gain.reshape(depth * 2, 1, d)
    b_ada3 = b_ada.reshape(depth, 1, 6 * d)
    ffn_b3 = ffn_b_conv.reshape(depth, 1, -1)
    mod0 = _ada(c, w_ada, b_ada3, 0).reshape(bsz, 1, 6 * d)
    mod1 = _ada(c, w_ada, b_ada3, 1).reshape(bsz, 1, 6 * d)
    kvmod = _ada(c, kv_w_ada[None], kv_b_ada.reshape(1, 1, 2 * d), 0).reshape(bsz, 1, 2 * d)

    qk_cols = 2 * RET_HEADS * RET_QK_DIM
    qkvg, w_out0, w_in0, w_down0 = _proj(
        xf, [(gains, 0, mod0, 1, 0, ret_w_in)], ret_cos, ret_sin, tm=PROJ_RET_TM, seq=seq,
        rope_mode="ret", rope_cols=qk_cols, silu_from=qk_cols + RET_HEADS * RET_V_DIM,
        staged=True, cast_jobs=[(ret_w_out, 0), (ffn_w_in, 0), (ffn_w_down, 0)])
    y = _retention(qkvg.reshape(bsz, seq, -1), bsz, seq).reshape(t, -1)
    xf, w_kv_bf, w_q_bf, w_out1, w_in1, w_down1 = _ffn(
        y, xf, gains, 1, mod0, w_out0, w_in0, ffn_w_conv, ffn_b3, w_down0,
        final_norm_gain[None], 0, seq=seq, final_norm=False,
        cast_jobs=[(w_kv[None], 0), (diff_w_q, 0), (diff_w_out, 0), (ffn_w_in, 1),
                   (ffn_w_down, 1)])

    k_dim = DIFF_HEADS * 2 * DIFF_HEAD_DIM
    kv, q = _proj(xf, [(kv_norm_gain.reshape(1, 1, d), 0, kvmod, 1, 0, w_kv_bf),
                       (gains, 2, mod1, 1, 0, w_q_bf)],
                  diff_cos, diff_sin, tm=PROJ_DIFF_TM, seq=seq, rope_mode="diff",
                  rope_cols=k_dim, staged=False)

    lam_init = 0.8 - 0.6 * math.exp(-0.3 * 1)
    o = _attention(q.reshape(bsz, seq, -1), kv.reshape(bsz, seq, -1), diff_lambda[0],
                   diff_subln_gain[0][None], lam_init=lam_init).reshape(t, -1)
    (xf,) = _ffn(o, xf, gains, 3, mod1, w_out1, w_in1, ffn_w_conv, ffn_b3, w_down1,
                 final_norm_gain[None], 1, seq=seq, final_norm=True)
    return xf.reshape(bsz, seq, d)
```

```python
import functools
import math

import jax
import jax.numpy as jnp
from jax import lax
from jax.experimental import pallas as pl
from jax.experimental.pallas import tpu as pltpu

F32 = jnp.float32
BF16 = jnp.bfloat16

RET_HEADS = 4
RET_QK_DIM = 256
RET_V_DIM = 512
DIFF_HEADS = 4
DIFF_HEAD_DIM = 128
DIFF_V_DIM = 256
CONV_WIDTH = 3
ROPE_THETA = 10000.0
NORM_EPS = 1e-6

LANES = 128
SUBLANES = 8
BF16_ROWS = 16
VMEM_LIMIT_BYTES = 56 * 1024 * 1024

ADA_TN = 1024
PROJ_RET_TM = 512
PROJ_DIFF_TM = 1024
PROJ_TN = 512
RET_KCHUNK = 256
RET_HEADS_PER_STEP = 2
FFN_TM = 512
FFN_TF = 256
ATT_TQ = 256
WEIGHT_STAGE_BYTES = 3 * 1024 * 1024
WEIGHT_STAGE_SLOTS = 2


def _cparams(semantics):
    return pltpu.CompilerParams(dimension_semantics=semantics,
                                vmem_limit_bytes=VMEM_LIMIT_BYTES)


def _rms(x):
    return x * lax.rsqrt(jnp.mean(x * x, axis=-1, keepdims=True) + NORM_EPS)


def _prenorm(x, gain, sc, sh):
    return (_rms(x) * gain) * (1.0 + sc) + sh


def _sigmoid(x):
    return 1.0 / (1.0 + jnp.exp(-x))


def _hbm_weight_spec():
    return pl.BlockSpec(memory_space=pl.ANY)


def _resident_spec(w):
    return pl.BlockSpec(w.shape, lambda *_: (0, 0))


def _cast_job_specs(jobs, n_steps):
    in_specs, out_specs, out_shapes = [], [], []
    for w, layer in jobs:
        _, k, n = w.shape
        reps = next(r for r in (1, 2, 4, 8) if (k * r) % (n_steps * BF16_ROWS) == 0)
        rows = k * reps // n_steps
        in_specs.append(pl.BlockSpec((None, rows, n), lambda i, l=layer, r=reps: (l, i // r, 0)))
        out_specs.append(pl.BlockSpec((rows, n), lambda i, r=reps: (i // r, 0)))
        out_shapes.append(jax.ShapeDtypeStruct((k, n), BF16))
    return in_specs, out_specs, out_shapes


def _run_cast_jobs(in_refs, out_refs):
    for src, dst in zip(in_refs, out_refs):
        dst[...] = src[...].astype(BF16)


def _stage_rows(k, n):
    return max(r for r in range(BF16_ROWS, k + 1, BF16_ROWS)
               if k % r == 0 and r * n * 4 <= WEIGHT_STAGE_BYTES)


def _staging_scratch(weights):
    n = weights[0].shape[2]
    assert all(w.shape[2] == n for w in weights)
    rows = max(_stage_rows(w.shape[1], n) for w in weights)
    return [pltpu.VMEM((WEIGHT_STAGE_SLOTS, rows, n), F32),
            pltpu.SemaphoreType.DMA((WEIGHT_STAGE_SLOTS,))]


def _bf16_weight_scratch(w):
    return pltpu.VMEM(w.shape[1:], BF16)


def _stage_weight_bf16(w_hbm, layer, dst_ref, stage_ref, sem):
    k, n = dst_ref.shape
    slots = stage_ref.shape[0]
    rows = _stage_rows(k, n)
    n_chunks = k // rows

    def chunk_copy(c):
        return pltpu.make_async_copy(w_hbm.at[layer, pl.ds(c * rows, rows), :],
                                     stage_ref.at[c % slots, pl.ds(0, rows), :],
                                     sem.at[c % slots])

    for c in range(min(slots - 1, n_chunks)):
        chunk_copy(c).start()
    for c in range(n_chunks):
        ahead = c + slots - 1
        if ahead < n_chunks:
            chunk_copy(ahead).start()
        chunk_copy(c).wait()
        dst_ref[c * rows:(c + 1) * rows, :] = stage_ref[c % slots, 0:rows, :].astype(BF16)


def _ada_kernel(c_ref, w_ref, b_ref, o_ref):
    c = c_ref[...]
    s = (c * _sigmoid(c)).astype(BF16)
    o_ref[...] = jnp.dot(s, w_ref[...].astype(BF16), preferred_element_type=F32) + b_ref[...]


def _ada(c, w, b, layer):
    bsz, d = c.shape
    n = w.shape[-1]
    return pl.pallas_call(
        _ada_kernel,
        grid=(n // ADA_TN,),
        in_specs=[
            pl.BlockSpec((bsz, d), lambda j: (0, 0)),
            pl.BlockSpec((None, d, ADA_TN), lambda j: (layer, 0, j)),
            pl.BlockSpec((None, 1, ADA_TN), lambda j: (layer, 0, j)),
        ],
        out_specs=pl.BlockSpec((bsz, ADA_TN), lambda j: (0, j)),
        out_shape=jax.ShapeDtypeStruct((bsz, n), F32),
        compiler_params=_cparams(("arbitrary",)),
        name="ada",
    )(c, w, b)


def _rope_ret(acc, lo, cos, sin, pos1, o_ref):
    qk_w = RET_HEADS * RET_QK_DIM
    for hh in range(PROJ_TN // RET_QK_DIM):
        c0 = hh * RET_QK_DIM
        col = lo + c0
        head = (col % qk_w) // RET_QK_DIM
        log_gamma = math.log(1.0 - 2.0 ** (-5.0 - head))
        if col < qk_w:
            dec = jnp.exp(pos1 * log_gamma)
        else:
            dec = jnp.exp(pos1 * (-log_gamma)) * (RET_QK_DIM ** -0.5)
        x1 = acc[:, c0:c0 + LANES]
        x2 = acc[:, c0 + LANES:c0 + 2 * LANES]
        o_ref[:, col:col + LANES] = ((x1 * cos - x2 * sin) * dec).astype(BF16)
        o_ref[:, col + LANES:col + 2 * LANES] = ((x2 * cos + x1 * sin) * dec).astype(BF16)


def _rope_diff(acc, lo, cos, sin, o_ref):
    for bb in range(PROJ_TN // LANES):
        c0 = bb * LANES
        xb = acc[:, c0:c0 + LANES]
        o_ref[:, lo + c0:lo + c0 + LANES] = (
            xb * cos + pltpu.roll(xb, LANES // 2, 1) * sin).astype(BF16)


def _proj_kernel(*refs, n_branch, n_cast, staged, rope_mode, rope_cols, silu_from):
    x_ref = refs[0]
    p = 1 + 4 * n_branch
    cos = refs[p][...]
    sin = refs[p + 1][...]
    cast_in = refs[p + 2:p + 2 + n_cast]
    out_refs = refs[p + 2 + n_cast:p + 2 + n_cast + n_branch]
    cast_out = refs[p + 2 + n_cast + n_branch:p + 2 + 2 * n_cast + n_branch]
    scratch = refs[p + 2 + 2 * n_cast + n_branch:]
    tm = x_ref.shape[0]

    if staged:
        @pl.when(pl.program_id(0) == 0)
        def _():
            for br in range(n_branch):
                _stage_weight_bf16(refs[4 + 4 * br], 0, *scratch[3 * br:3 * br + 3])

    _run_cast_jobs(cast_in, cast_out)

    normed = _rms(x_ref[...])
    if rope_mode == "ret":
        row = lax.broadcasted_iota(jnp.int32, (tm, 1), 0)
        pos1 = (jnp.bitwise_and(row, RET_KCHUNK - 1) + 1).astype(F32)
    for br in range(n_branch):
        gain_ref, sc_ref, sh_ref = refs[1 + 4 * br:4 + 4 * br]
        w_ref = scratch[3 * br] if staged else refs[4 + 4 * br]
        o_ref = out_refs[br]
        h = ((normed * gain_ref[...]) * (1.0 + sc_ref[...]) + sh_ref[...]).astype(BF16)
        for j in range(w_ref.shape[1] // PROJ_TN):
            lo = j * PROJ_TN
            acc = jnp.dot(h, w_ref[:, lo:lo + PROJ_TN], preferred_element_type=F32)
            if lo >= silu_from:
                o_ref[:, lo:lo + PROJ_TN] = (acc * _sigmoid(acc)).astype(BF16)
            elif lo >= rope_cols:
                o_ref[:, lo:lo + PROJ_TN] = acc.astype(BF16)
            elif rope_mode == "ret":
                _rope_ret(acc, lo, cos, sin, pos1, o_ref)
            else:
                _rope_diff(acc, lo, cos, sin, o_ref)


def _proj(x, branches, cos, sin, *, tm, seq, rope_mode, rope_cols, staged, cast_jobs=(),
          silu_from=None):
    t, d = x.shape
    tiles_per_seq = seq // tm
    n_steps = t // tm
    in_specs = [pl.BlockSpec((tm, d), lambda i: (i, 0))]
    args = [x]
    out_specs, out_shapes, scratch = [], [], []
    for gains, gain_idx, mod, sc_idx, sh_idx, w in branches:
        n = w.shape[-1]
        in_specs += [
            pl.BlockSpec((None, 1, d), lambda i, g=gain_idx: (g, 0, 0)),
            pl.BlockSpec((None, 1, d), lambda i, c=sc_idx: (i // tiles_per_seq, 0, c)),
            pl.BlockSpec((None, 1, d), lambda i, c=sh_idx: (i // tiles_per_seq, 0, c)),
            _hbm_weight_spec() if staged else _resident_spec(w),
        ]
        args += [gains, mod, mod, w]
        if staged:
            scratch += [_bf16_weight_scratch(w)] + _staging_scratch([w])
        out_specs.append(pl.BlockSpec((tm, n), lambda i: (i, 0)))
        out_shapes.append(jax.ShapeDtypeStruct((t, n), BF16))
    in_specs += [pl.BlockSpec((tm, LANES), lambda i: (i % tiles_per_seq, 0))] * 2
    args += [cos, sin]
    cast_in, cast_out, cast_shapes = _cast_job_specs(cast_jobs, n_steps)
    in_specs += cast_in
    args += [w for w, _ in cast_jobs]
    if silu_from is None:
        silu_from = max(w.shape[-1] for *_, w in branches)
    kern = functools.partial(_proj_kernel, n_branch=len(branches), n_cast=len(cast_jobs),
                             staged=staged, rope_mode=rope_mode, rope_cols=rope_cols,
                             silu_from=silu_from)
    return pl.pallas_call(
        kern,
        grid=(n_steps,),
        in_specs=in_specs,
        out_specs=out_specs + cast_out,
        out_shape=out_shapes + cast_shapes,
        scratch_shapes=scratch,
        compiler_params=_cparams(("arbitrary",)),
        name="proj_" + rope_mode,
    )(*args)


def _ret_kernel(q_ref, k_ref, v_ref, o_ref):
    seq = q_ref.shape[0]
    chunk = RET_KCHUNK
    dk, dv = RET_QK_DIM, RET_V_DIM
    row = lax.broadcasted_iota(jnp.int32, (chunk, chunk), 0)
    col = lax.broadcasted_iota(jnp.int32, (chunk, chunk), 1)
    causal = col <= row
    n_chunks = seq // chunk

    for j in range(RET_HEADS_PER_STEP):
        hf = (pl.program_id(1) * RET_HEADS_PER_STEP + j).astype(F32)
        log_gamma = jnp.log(1.0 - jnp.exp2(jnp.full((1, 1), -5.0, F32) - hf))
        c_dec = jnp.exp(log_gamma * float(chunk))

        def intra(n):
            r0 = n * chunk
            qc = q_ref[r0:r0 + chunk, j * dk:(j + 1) * dk]
            kc = k_ref[r0:r0 + chunk, j * dk:(j + 1) * dk]
            vc = v_ref[r0:r0 + chunk, j * dv:(j + 1) * dv]
            s = lax.dot_general(qc, kc, (((1,), (1,)), ((), ())), preferred_element_type=F32)
            inner = jnp.where(causal, s, 0.0).astype(BF16)
            out = jnp.dot(inner, vc, preferred_element_type=F32)
            kv = None
            if n + 1 < n_chunks:
                kv = lax.dot_general(kc, vc, (((0,), (0,)), ((), ())),
                                     preferred_element_type=F32)
            return out, kv

        state = None
        nxt = intra(0)
        for n in range(n_chunks):
            r0 = n * chunk
            out, kv = nxt
            if n + 1 < n_chunks:
                nxt = intra(n + 1)
            if state is not None:
                out = out + jnp.dot(q_ref[r0:r0 + chunk, j * dk:(j + 1) * dk],
                                    state.astype(BF16), preferred_element_type=F32)
            if kv is not None:
                state = c_dec * (kv if state is None else state + kv)
            o_ref[r0:r0 + chunk, j * dv:(j + 1) * dv] = _rms(out).astype(BF16)


def _retention(qkvg, bsz, seq):
    hps = RET_HEADS_PER_STEP
    dk = hps * RET_QK_DIM
    dv = hps * RET_V_DIM
    groups = RET_HEADS // hps
    qk_blocks = RET_HEADS * RET_QK_DIM // dv
    return pl.pallas_call(
        _ret_kernel,
        grid=(bsz, groups),
        in_specs=[
            pl.BlockSpec((None, seq, dk), lambda b, h: (b, 0, h)),
            pl.BlockSpec((None, seq, dk), lambda b, h: (b, 0, groups + h)),
            pl.BlockSpec((None, seq, dv), lambda b, h: (b, 0, 2 * qk_blocks + h)),
        ],
        out_specs=pl.BlockSpec((None, seq, dv), lambda b, h: (b, 0, h)),
        out_shape=jax.ShapeDtypeStruct((bsz, seq, RET_HEADS * RET_V_DIM), BF16),
        compiler_params=_cparams(("arbitrary", "arbitrary")),
        name="retention",
    )(qkvg, qkvg, qkvg)


def _ffn_kernel(*refs, n_gate, n_cast, tiles_per_seq, final_norm):
    (mix_ref, x_ref, g1_ref, gain_ref, sc_ref, sh_ref, g2_ref, wout_ref, win_ref,
     wconv_ref, bconv_ref, wdown_ref, fgain_ref) = refs[:13]
    p = 13 + n_gate
    cast_in = refs[p:p + n_cast]
    o_ref = refs[p + n_cast]
    cast_out = refs[p + 1 + n_cast:p + 1 + 2 * n_cast]
    carry_ref, y_ref = refs[p + 1 + 2 * n_cast:]
    tm = x_ref.shape[0]
    dff = wdown_ref.shape[0]

    _run_cast_jobs(cast_in, cast_out)

    mix = mix_ref[...]
    if n_gate:
        mix = (mix.astype(F32) * refs[13][...].astype(F32)).astype(BF16)
    x = x_ref[...] + g1_ref[...] * jnp.dot(mix, wout_ref[...], preferred_element_type=F32)
    h = _prenorm(x, gain_ref[...], sc_ref[...], sh_ref[...]).astype(BF16)
    seq_start = (pl.program_id(0) % tiles_per_seq) == 0
    sqrt_half = math.sqrt(0.5)

    for f in range(dff // FFN_TF):
        lo = f * FFN_TF
        a = jnp.dot(h, win_ref[:, lo:lo + FFN_TF], preferred_element_type=F32)
        g = jnp.dot(h, win_ref[:, dff + lo:dff + lo + FFN_TF], preferred_element_type=F32)
        w0 = wconv_ref[0:1, lo:lo + FFN_TF]
        w1 = wconv_ref[1:2, lo:lo + FFN_TF]
        w2 = wconv_ref[2:3, lo:lo + FFN_TF]
        bias = bconv_ref[:, lo:lo + FFN_TF]

        def conv_gelu_gate(a0, a1, a2, gate):
            ac = w0 * a2 + w1 * a1 + w2 * a0 + bias
            return (0.5 * ac * (1.0 + lax.erf(ac * sqrt_half))) * gate

        y_ref[:, lo:lo + FFN_TF] = conv_gelu_gate(
            a, pltpu.roll(a, 1, 0), pltpu.roll(a, 2, 0), g).astype(BF16)
        prev = jnp.where(seq_start, 0.0, carry_ref[:, lo:lo + FFN_TF])
        ext = jnp.concatenate([prev, a[0:BF16_ROWS]], axis=0)
        top = conv_gelu_gate(a[0:BF16_ROWS],
                             pltpu.roll(ext, 1, 0)[SUBLANES:],
                             pltpu.roll(ext, 2, 0)[SUBLANES:],
                             g[0:BF16_ROWS])
        y_ref[0:BF16_ROWS, lo:lo + FFN_TF] = top.astype(BF16)
        carry_ref[:, lo:lo + FFN_TF] = a[tm - SUBLANES:tm]

    out = x + g2_ref[...] * jnp.dot(y_ref[...], wdown_ref[...], preferred_element_type=F32)
    if final_norm:
        out = _rms(out) * fgain_ref[...]
    o_ref[...] = out


def _ffn(mix, x, gains, gain_idx, mod, w_out, w_in, w_conv, b_conv, w_down, fgain, layer, *,
         seq, final_norm, cast_jobs=(), gate=None):
    t, d = x.shape
    kmix = mix.shape[1]
    dff = w_down.shape[0]
    tiles_per_seq = seq // FFN_TM
    n_steps = t // FFN_TM
    cast_in, cast_out, cast_shapes = _cast_job_specs(cast_jobs, n_steps)
    gate_specs, gate_args = [], []
    if gate is not None:
        gate_specs = [pl.BlockSpec((FFN_TM, kmix), lambda i, c=gate[1]: (i, c))]
        gate_args = [gate[0]]
    kern = functools.partial(_ffn_kernel, n_gate=len(gate_args), n_cast=len(cast_jobs),
                             tiles_per_seq=tiles_per_seq, final_norm=final_norm)
    return pl.pallas_call(
        kern,
        grid=(n_steps,),
        in_specs=[
            pl.BlockSpec((FFN_TM, kmix), lambda i: (i, 0)),
            pl.BlockSpec((FFN_TM, d), lambda i: (i, 0)),
            pl.BlockSpec((None, 1, d), lambda i: (i // tiles_per_seq, 0, 2)),
            pl.BlockSpec((None, 1, d), lambda i: (gain_idx, 0, 0)),
            pl.BlockSpec((None, 1, d), lambda i: (i // tiles_per_seq, 0, 4)),
            pl.BlockSpec((None, 1, d), lambda i: (i // tiles_per_seq, 0, 3)),
            pl.BlockSpec((None, 1, d), lambda i: (i // tiles_per_seq, 0, 5)),
            _resident_spec(w_out),
            _resident_spec(w_in),
            pl.BlockSpec((None, CONV_WIDTH, dff), lambda i: (layer, 0, 0)),
            pl.BlockSpec((None, 1, dff), lambda i: (layer, 0, 0)),
            _resident_spec(w_down),
            pl.BlockSpec((1, d), lambda i: (0, 0)),
        ] + gate_specs + cast_in,
        out_specs=[pl.BlockSpec((FFN_TM, d), lambda i: (i, 0))] + cast_out,
        out_shape=[jax.ShapeDtypeStruct((t, d), F32)] + cast_shapes,
        scratch_shapes=[
            pltpu.VMEM((SUBLANES, dff), F32),
            pltpu.VMEM((FFN_TM, dff), BF16),
        ],
        compiler_params=_cparams(("arbitrary",)),
        name="convffn",
    )(mix, x, mod, gains, mod, mod, mod, w_out, w_in, w_conv, b_conv, w_down, fgain,
      *gate_args, *[w for w, _ in cast_jobs])


def _attn_kernel(lam_ref, q_ref, k_ref, v_ref, gain_ref, o_ref, *, lam_init):
    seq = q_ref.shape[0]
    tq = ATT_TQ
    hd = DIFF_HEAD_DIM
    exp_scale = (hd ** -0.5) * math.log2(math.e)
    contract_last = (((1,), (1,)), ((), ()))

    lv = lam_ref[...]
    lam = (jnp.exp(jnp.sum(lv[0:1] * lv[1:2], axis=1, keepdims=True))
           - jnp.exp(jnp.sum(lv[2:3] * lv[3:4], axis=1, keepdims=True)) + lam_init)

    row = lax.broadcasted_iota(jnp.int32, (tq, tq), 0)
    col = lax.broadcasted_iota(jnp.int32, (tq, tq), 1)
    causal = col <= row

    items = [(qi, half) for qi in reversed(range(seq // tq)) for half in range(2)]
    scores, probs, outs = {}, {}, {}

    def stage_scores(w):
        qi, half = items[w]
        r0, c0 = qi * tq, half * hd
        qh = q_ref[r0:r0 + tq, c0:c0 + hd]
        s_diag = lax.dot_general(qh, k_ref[r0:r0 + tq, c0:c0 + hd], contract_last,
                                 preferred_element_type=F32)
        s_diag = jnp.where(causal, s_diag, -jnp.inf)
        m = jnp.max(s_diag, axis=-1, keepdims=True)
        s_main = None
        if qi > 0:
            s_main = lax.dot_general(qh, k_ref[0:r0, c0:c0 + hd], contract_last,
                                     preferred_element_type=F32)
            m = jnp.maximum(m, jnp.max(s_main, axis=-1, keepdims=True))
        scores[w] = (s_diag, s_main, m)

    def stage_probs(w):
        s_diag, s_main, m = scores.pop(w)
        p_diag = jnp.exp2((s_diag - m) * exp_scale)
        l = jnp.sum(p_diag, axis=-1, keepdims=True)
        p_main = None
        if s_main is not None:
            p_main = jnp.exp2((s_main - m) * exp_scale)
            l = l + jnp.sum(p_main, axis=-1, keepdims=True)
            p_main = p_main.astype(BF16)
        probs[w] = (p_diag.astype(BF16), p_main, l)

    def stage_values(w):
        qi, half = items[w]
        r0 = qi * tq
        p_diag, p_main, l = probs.pop(w)
        acc = jnp.dot(p_diag, v_ref[r0:r0 + tq, :], preferred_element_type=F32)
        if p_main is not None:
            acc = acc + jnp.dot(p_main, v_ref[0:r0, :], preferred_element_type=F32)
        outs[w] = acc * (1.0 / l)
        if half == 1:
            o = outs.pop(w - 1) - lam * outs.pop(w)
            o_ref[r0:r0 + tq, :] = ((_rms(o) * gain_ref[...]) * (1.0 - lam_init)).astype(BF16)

    n_q = len(items) // 2
    for t in range(n_q + 2):
        for half in range(2):
            if t < n_q:
                stage_scores(2 * t + half)
        for half in range(2):
            if 0 <= t - 1 < n_q:
                stage_probs(2 * (t - 1) + half)
        for half in range(2):
            if 0 <= t - 2 < n_q:
                stage_values(2 * (t - 2) + half)


def _attention(q, kv, lam_params, subln_gain, *, lam_init):
    bsz, seq, _ = q.shape
    w = 2 * DIFF_HEAD_DIM
    kern = functools.partial(_attn_kernel, lam_init=lam_init)
    return pl.pallas_call(
        kern,
        grid=(bsz, DIFF_HEADS),
        in_specs=[
            pl.BlockSpec((4, DIFF_HEAD_DIM), lambda b, h: (0, 0)),
            pl.BlockSpec((None, seq, w), lambda b, h: (b, 0, h)),
            pl.BlockSpec((None, seq, w), lambda b, h: (b, 0, h)),
            pl.BlockSpec((None, seq, DIFF_V_DIM), lambda b, h: (b, 0, DIFF_HEADS + h)),
            pl.BlockSpec((1, DIFF_V_DIM), lambda b, h: (0, 0)),
        ],
        out_specs=pl.BlockSpec((None, seq, DIFF_V_DIM), lambda b, h: (b, 0, h)),
        out_shape=jax.ShapeDtypeStruct((bsz, seq, DIFF_HEADS * DIFF_V_DIM), BF16),
        compiler_params=_cparams(("arbitrary", "arbitrary")),
        name="diffattn",
    )(lam_params, q, kv, kv, subln_gain)


def kernel(x, c, norm_gain, w_ada, b_ada, ret_w_in, ret_w_out, ffn_w_in, ffn_w_conv,
           ffn_b_conv, ffn_w_down, kv_norm_gain, kv_w_ada, kv_b_ada, w_kv, diff_w_q,
           diff_lambda, diff_subln_gain, diff_w_out, final_norm_gain):
    bsz, seq, d = x.shape
    t = bsz * seq
    depth = w_ada.shape[0]
    n_a = ret_w_in.shape[0]
    assert depth == 2 and n_a == 1 and diff_w_q.shape[0] == 1
    assert seq % PROJ_RET_TM == 0 and seq % PROJ_DIFF_TM == 0 and seq % FFN_TM == 0
    assert seq % ATT_TQ == 0 and PROJ_RET_TM % RET_KCHUNK == 0
    assert RET_KCHUNK & (RET_KCHUNK - 1) == 0

    pos = jnp.arange(seq, dtype=F32)[:, None]
    ret_freqs = 1.0 / (ROPE_THETA ** jnp.linspace(0.0, 1.0, LANES))
    ret_ang = pos * ret_freqs[None, :]
    ret_cos, ret_sin = jnp.cos(ret_ang), jnp.sin(ret_ang)
    diff_freqs = 1.0 / (ROPE_THETA ** (jnp.arange(0, DIFF_HEAD_DIM, 2, dtype=F32) / DIFF_HEAD_DIM))
    diff_ang = pos * diff_freqs[None, :]
    dcos, dsin = jnp.cos(diff_ang), jnp.sin(diff_ang)
    diff_cos = jnp.concatenate([dcos, dcos], axis=1)
    diff_sin = jnp.concatenate([-dsin, dsin], axis=1)

    xf = x.reshape(t, d)
    gains = norm_gain.reshape(depth * 2, 1, d)
    b_ada3 = b_ada.reshape(depth, 1, 6 * d)
    ffn_b3 = ffn_b_conv.reshape(depth, 1, -1)
    mod0 = _ada(c, w_ada, b_ada3, 0).reshape(bsz, 1, 6 * d)
    mod1 = _ada(c, w_ada, b_ada3, 1).reshape(bsz, 1, 6 * d)
    kvmod = _ada(c, kv_w_ada[None], kv_b_ada.reshape(1, 1, 2 * d), 0).reshape(bsz, 1, 2 * d)

    qk_cols = 2 * RET_HEADS * RET_QK_DIM
    qkvg, w_out0, w_in0, w_down0 = _proj(
        xf, [(gains, 0, mod0, 1, 0, ret_w_in)], ret_cos, ret_sin, tm=PROJ_RET_TM, seq=seq,
        rope_mode="ret", rope_cols=qk_cols, silu_from=qk_cols + RET_HEADS * RET_V_DIM,
        staged=True, cast_jobs=[(ret_w_out, 0), (ffn_w_in, 0), (ffn_w_down, 0)])
    y = _retention(qkvg.reshape(bsz, seq, -1), bsz, seq).reshape(t, -1)
    xf, w_kv_bf, w_q_bf, w_out1, w_in1, w_down1 = _ffn(
        y, xf, gains, 1, mod0, w_out0, w_in0, ffn_w_conv, ffn_b3, w_down0,
        final_norm_gain[None], 0, seq=seq, final_norm=False,
        gate=(qkvg, (qk_cols + RET_HEADS * RET_V_DIM) // (RET_HEADS * RET_V_DIM)),
        cast_jobs=[(w_kv[None], 0), (diff_w_q, 0), (diff_w_out, 0), (ffn_w_in, 1),
                   (ffn_w_down, 1)])

    k_dim = DIFF_HEADS * 2 * DIFF_HEAD_DIM
    kv, q = _proj(xf, [(kv_norm_gain.reshape(1, 1, d), 0, kvmod, 1, 0, w_kv_bf),
                       (gains, 2, mod1, 1, 0, w_q_bf)],
                  diff_cos, diff_sin, tm=PROJ_DIFF_TM, seq=seq, rope_mode="diff",
                  rope_cols=k_dim, staged=False)

    lam_init = 0.8 - 0.6 * math.exp(-0.3 * 1)
    o = _attention(q.reshape(bsz, seq, -1), kv.reshape(bsz, seq, -1), diff_lambda[0],
                   diff_subln_gain[0][None], lam_init=lam_init).reshape(t, -1)
    (xf,) = _ffn(o, xf, gains, 3, mod1, w_out1, w_in1, ffn_w_conv, ffn_b3, w_down1,
                 final_norm_gain[None], 1, seq=seq, final_norm=True)
    return xf.reshape(bsz, seq, d)
```
